```python
import math
import jax, jax.numpy as jnp
from jax import lax
import numpy as np

D_MODEL = 1024
BATCH = 2
SEQ = 8192
DEPTH = 2

MLA_HEADS = 4
MLA_Q_LORA = 256
MLA_KV_LORA = 128
MLA_D_NOPE = 128
MLA_D_ROPE = 64
MLA_D_V = 128
ROPE_THETA = 10000.0

DIFF_HEADS = 4
DIFF_D_QK = 64
DIFF_D_V = 2 * DIFF_D_QK

SSD_HEADS = 16
SSD_HEAD_DIM = 64
SSD_D_INNER = SSD_HEADS * SSD_HEAD_DIM
SSD_GROUPS = 4
SSD_STATE = 128
SSD_CONV = 4
SSD_CHUNK = 128
SSD_CONV_DIM = SSD_D_INNER + 2 * SSD_GROUPS * SSD_STATE

D_MIX = MLA_HEADS * MLA_D_V + DIFF_HEADS * DIFF_D_V + SSD_D_INNER

IN_SIZES = (MLA_Q_LORA, MLA_KV_LORA, MLA_D_ROPE,
            DIFF_HEADS * 2 * DIFF_D_QK, DIFF_HEADS * 2 * DIFF_D_QK, DIFF_HEADS * DIFF_D_V,
            SSD_D_INNER, SSD_CONV_DIM, SSD_HEADS)
N_IN = sum(IN_SIZES)

MOE_GROUPS = 4
MOE_EXPERTS_PER_GROUP = 8
N_EXPERTS = MOE_GROUPS * MOE_EXPERTS_PER_GROUP
MOE_TOP_K = 2
D_EXPERT = 512
MOE_BLOCK = 128

Q_BLOCK = 128
NORM_EPS = 1e-6
SUBLN_EPS = 1e-5

kernel_name = 'hybrid_mla_diff_ssd_hmoe'


def rmsnorm(x, w, eps=NORM_EPS):
    xf = x.astype(jnp.float32)
    y = xf * lax.rsqrt(jnp.mean(xf * xf, axis=-1, keepdims=True) + eps)
    return (y * w.astype(jnp.float32)).astype(x.dtype)


def modulate(h, shift, scale):
    return h * (1 + scale[:, None, :]) + shift[:, None, :]


def rope_tables(positions, dim, dtype):
    inv = 1.0 / (ROPE_THETA ** (jnp.arange(0, dim, 2, dtype=jnp.float32) / dim))
    ang = positions.astype(jnp.float32)[..., None] * inv
    ang = jnp.concatenate([ang, ang], axis=-1)
    return jnp.cos(ang).astype(dtype), jnp.sin(ang).astype(dtype)


def apply_rope(t, cos, sin):
    half = t.shape[-1] // 2
    rot = jnp.concatenate([-t[..., half:], t[..., :half]], axis=-1)
    return t * cos + rot * sin


def to_query_blocks(t):
    nb = t.shape[1] // Q_BLOCK
    return t.reshape(t.shape[0], nb, Q_BLOCK, *t.shape[2:]).swapaxes(0, 1)


def from_query_blocks(t):
    t = t.swapaxes(0, 1)
    return t.reshape(t.shape[0], t.shape[1] * t.shape[2], *t.shape[3:])


def causal_block_mask(i, s):
    q_idx = i * Q_BLOCK + jnp.arange(Q_BLOCK)
    return q_idx[:, None] >= jnp.arange(s)[None, :]


def mla_group(cq, ckv, k_pe, positions, q_norm_w, w_uq, kv_norm_w, w_ukv):
    bsz, s, _ = cq.shape
    q = (rmsnorm(cq, q_norm_w) @ w_uq).reshape(bsz, s, MLA_HEADS, MLA_D_NOPE + MLA_D_ROPE)
    kv = (rmsnorm(ckv, kv_norm_w) @ w_ukv).reshape(bsz, s, MLA_HEADS, MLA_D_NOPE + MLA_D_V)
    q_nope, q_pe = q[..., :MLA_D_NOPE], q[..., MLA_D_NOPE:]
    k_nope, v = kv[..., :MLA_D_NOPE], kv[..., MLA_D_NOPE:]
    cos, sin = rope_tables(positions, MLA_D_ROPE, cq.dtype)
    q_pe = apply_rope(q_pe, cos[:, :, None], sin[:, :, None])
    k_pe = apply_rope(k_pe, cos, sin)
    scale = (MLA_D_NOPE + MLA_D_ROPE) ** -0.5

    def block(args):
        qn, qp, i = args
        sc = (jnp.einsum('bqhd,bkhd->bhqk', qn, k_nope)
              + jnp.einsum('bqhd,bkd->bhqk', qp, k_pe)).astype(jnp.float32) * scale
        sc = jnp.where(causal_block_mask(i, s), sc, -jnp.inf)
        p = jax.nn.softmax(sc, axis=-1).astype(v.dtype)
        return jnp.einsum('bhqk,bkhd->bqhd', p, v)

    nb = s // Q_BLOCK
    out = lax.map(block, (to_query_blocks(q_nope), to_query_blocks(q_pe), jnp.arange(nb)))
    return from_query_blocks(out).reshape(bsz, s, MLA_HEADS * MLA_D_V)


def diff_group(q, k, v, positions, lambdas, subln_w, layer_idx):
    bsz, s, _ = q.shape
    q = q.reshape(bsz, s, DIFF_HEADS, 2, DIFF_D_QK)
    k = k.reshape(bsz, s, DIFF_HEADS, 2, DIFF_D_QK)
    v = v.reshape(bsz, s, DIFF_HEADS, DIFF_D_V)
    lam_init = 0.8 - 0.6 * math.exp(-0.3 * layer_idx)
    lf = lambdas.astype(jnp.float32)
    lam = jnp.exp(jnp.sum(lf[0] * lf[1])) - jnp.exp(jnp.sum(lf[2] * lf[3])) + lam_init
    slopes = jnp.exp2(-8.0 / DIFF_HEADS * jnp.arange(1, DIFF_HEADS + 1, dtype=jnp.float32))
    pos_k = positions.astype(jnp.float32)
    scale = DIFF_D_QK ** -0.5

    def block(args):
        qb, pq, i = args
        sc = jnp.einsum('bqhmd,bkhmd->bhmqk', qb, k).astype(jnp.float32) * scale
        dist = jnp.abs(pq.astype(jnp.float32)[:, :, None] - pos_k[:, None, :])
        sc = sc - slopes[None, :, None, None, None] * dist[:, None, None]
        sc = jnp.where(causal_block_mask(i, s), sc, -jnp.inf)
        p = jax.nn.softmax(sc, axis=-1)
        attn = (p[:, :, 0] - lam * p[:, :, 1]).astype(v.dtype)
        return jnp.einsum('bhqk,bkhe->bqhe', attn, v)

    nb = s // Q_BLOCK
    out = from_query_blocks(lax.map(block, (to_query_blocks(q), to_query_blocks(positions), jnp.arange(nb))))
    out = rmsnorm(out, subln_w, SUBLN_EPS) * (1.0 - lam_init)
    return out.reshape(bsz, s, DIFF_HEADS * DIFF_D_V)


def causal_depthwise_conv(u, w, b):
    y = lax.conv_general_dilated(u, w[:, None, :], window_strides=(1,), padding=[(SSD_CONV - 1, 0)],
                                 dimension_numbers=('NWC', 'WIO', 'NWC'), feature_group_count=u.shape[-1])
    return y + b


def ssd_chunked(x, dt, a, bm, cm):
    bsz, s, nh, p = x.shape
    nc = s // SSD_CHUNK
    r = nh // SSD_GROUPS
    xdt = (x * dt[..., None].astype(x.dtype)).reshape(bsz, nc, SSD_CHUNK, SSD_GROUPS, r, p)
    adt = (dt * a).reshape(bsz, nc, SSD_CHUNK, SSD_GROUPS, r).transpose(0, 3, 4, 1, 2)
    bc = bm.reshape(bsz, nc, SSD_CHUNK, SSD_GROUPS, SSD_STATE)
    cc = cm.reshape(bsz, nc, SSD_CHUNK, SSD_GROUPS, SSD_STATE)
    a_cs = jnp.cumsum(adt, axis=-1)
    tril = jnp.tril(jnp.ones((SSD_CHUNK, SSD_CHUNK), dtype=bool))
    seg = a_cs[..., :, None] - a_cs[..., None, :]
    decay_in = jnp.exp(jnp.where(tril, seg, -jnp.inf)).astype(x.dtype)
    cb = jnp.einsum('bclgn,bcsgn->bcgls', cc, bc)
    y_diag = jnp.einsum('bcgls,bgrcls,bcsgrp->bclgrp', cb, decay_in, xdt)
    decay_to_end = jnp.exp(a_cs[..., -1:] - a_cs).astype(x.dtype)
    states = jnp.einsum('bclgn,bgrcl,bclgrp->bcgrpn', bc, decay_to_end, xdt)
    chunk_tot = jnp.pad(a_cs[..., -1], ((0, 0), (0, 0), (0, 0), (1, 0)))
    ccs = jnp.cumsum(chunk_tot, axis=-1)
    trilc = jnp.tril(jnp.ones((nc + 1, nc + 1), dtype=bool))
    decay_chunk = jnp.exp(jnp.where(trilc, ccs[..., :, None] - ccs[..., None, :], -jnp.inf)).astype(x.dtype)
    states = jnp.concatenate([jnp.zeros_like(states[:, :1]), states], axis=1)
    entering = jnp.einsum('bgrzk,bkgrpn->bzgrpn', decay_chunk, states)[:, :-1]
    y_off = jnp.einsum('bclgn,bcgrpn,bgrcl->bclgrp', cc, entering, jnp.exp(a_cs).astype(x.dtype))
    return (y_diag + y_off).reshape(bsz, s, nh, p)


def ssd_group(z, xbc, dt_raw, conv_w, conv_b, dt_bias, a_log, d_skip, norm_w):
    bsz, s, _ = z.shape
    xbc = jax.nn.silu(causal_depthwise_conv(xbc, conv_w, conv_b))
    xs, bm, cm = jnp.split(xbc, [SSD_D_INNER, SSD_D_INNER + SSD_GROUPS * SSD_STATE], axis=-1)
    xs = xs.reshape(bsz, s, SSD_HEADS, SSD_HEAD_DIM)
    bm = bm.reshape(bsz, s, SSD_GROUPS, SSD_STATE)
    cm = cm.reshape(bsz, s, SSD_GROUPS, SSD_STATE)
    dt = jax.nn.softplus(dt_raw.astype(jnp.float32) + dt_bias.astype(jnp.float32))
    a = -jnp.exp(a_log.astype(jnp.float32))
    y = ssd_chunked(xs, dt, a, bm, cm) + xs * d_skip[:, None]
    y = (y.reshape(bsz, s, SSD_D_INNER) * jax.nn.silu(z)).reshape(bsz, s, SSD_GROUPS, SSD_D_INNER // SSD_GROUPS)
    y = rmsnorm(y, norm_w.reshape(SSD_GROUPS, SSD_D_INNER // SSD_GROUPS))
    return y.reshape(bsz, s, SSD_D_INNER)


def hybrid_mixer(h, positions, layer_idx, w_in, mla_q_norm_w, mla_w_uq, mla_kv_norm_w, mla_w_ukv,
                 diff_lambda, diff_subln_w, ssd_conv_w, ssd_conv_b, ssd_dt_bias, ssd_a_log, ssd_d,
                 ssd_norm_w, w_out):
    proj = h @ w_in
    split_at = [int(o) for o in np.cumsum(IN_SIZES)[:-1]]
    cq, ckv, k_pe, dq, dk, dv, z, xbc, dt = jnp.split(proj, split_at, axis=-1)
    y_mla = mla_group(cq, ckv, k_pe, positions, mla_q_norm_w, mla_w_uq, mla_kv_norm_w, mla_w_ukv)
    y_diff = diff_group(dq, dk, dv, positions, diff_lambda, diff_subln_w, layer_idx)
    y_ssd = ssd_group(z, xbc, dt, ssd_conv_w, ssd_conv_b, ssd_dt_bias, ssd_a_log, ssd_d, ssd_norm_w)
    return jnp.concatenate([y_mla, y_diff, y_ssd], axis=-1) @ w_out


def hier_moe(h, w_group, b_group, w_expert, b_expert, w_gate, w_up, w_down):
    bsz, s, d = h.shape
    t = bsz * s
    hf = h.reshape(t, d)
    pg = jax.nn.softmax((hf @ w_group + b_group).astype(jnp.float32), axis=-1)
    pg_top, g_sel = lax.top_k(pg, 1)
    le = (hf @ w_expert + b_expert).astype(jnp.float32).reshape(t, MOE_GROUPS, MOE_EXPERTS_PER_GROUP)
    le_sel = jnp.take_along_axis(le, g_sel[:, :, None], axis=1)[:, 0]
    pe_top, e_top = lax.top_k(jax.nn.softmax(le_sel, axis=-1), MOE_TOP_K)
    gate_w = pg_top * pe_top / jnp.sum(pe_top, axis=-1, keepdims=True)
    eid = g_sel * MOE_EXPERTS_PER_GROUP + e_top
    flat_e = eid.reshape(-1)
    flat_w = gate_w.reshape(-1)
    flat_tok = jnp.repeat(jnp.arange(t, dtype=jnp.int32), MOE_TOP_K)
    order = jnp.argsort(flat_e)
    se, stok, sw = flat_e[order], flat_tok[order], flat_w[order]
    counts = jnp.bincount(flat_e, length=N_EXPERTS)
    pcounts = (counts + MOE_BLOCK - 1) // MOE_BLOCK * MOE_BLOCK
    pend = jnp.cumsum(pcounts)
    pstart = pend - pcounts
    start = jnp.cumsum(counts) - counts
    dest = pstart[se] + jnp.arange(t * MOE_TOP_K) - start[se]
    cap = t * MOE_TOP_K + N_EXPERTS * MOE_BLOCK
    nb = cap // MOE_BLOCK
    btok = jnp.zeros((cap,), jnp.int32).at[dest].set(stok)
    bw = jnp.zeros((cap,), jnp.float32).at[dest].set(sw)
    bexp = jnp.clip(jnp.searchsorted(pend, jnp.arange(nb) * MOE_BLOCK, side='right'), 0, N_EXPERTS - 1)
    xb = hf[btok].reshape(nb, MOE_BLOCK, d)

    def expert_block(args):
        xblk, e = args
        return (jax.nn.silu(xblk @ w_gate[e]) * (xblk @ w_up[e])) @ w_down[e]

    yb = lax.map(expert_block, (xb, bexp)).reshape(cap, d)
    y = jnp.zeros_like(hf).at[btok].add(yb * bw[:, None].astype(yb.dtype))
    return y.reshape(bsz, s, d)


def setup_inputs(seed: int = 0) -> dict:
    key = jax.random.key(seed)
    ks = jax.random.split(key, 32)
    L = DEPTH

    def nrm(k, shape, scale):
        return jax.random.normal(k, shape, jnp.float32) * scale

    def gain(k, shape):
        return 1.0 + 0.05 * jax.random.normal(k, shape, jnp.float32)

    dt0 = jnp.exp(jax.random.uniform(ks[16], (L, SSD_HEADS), jnp.float32, math.log(1e-3), math.log(1e-1)))
    return {
        'x': nrm(ks[0], (BATCH, SEQ, D_MODEL), 1.0),
        'c': nrm(ks[1], (BATCH, D_MODEL), 1.0),
        'positions': jnp.broadcast_to(jnp.arange(SEQ, dtype=jnp.int32), (BATCH, SEQ)),
        'w_ada': nrm(ks[2], (L, D_MODEL, 6 * D_MODEL), D_MODEL ** -0.5),
        'b_ada': nrm(ks[3], (L, 6 * D_MODEL), 0.02),
        'norm1_w': gain(ks[4], (L, D_MODEL)),
        'w_in': nrm(ks[5], (L, D_MODEL, N_IN), D_MODEL ** -0.5),
        'mla_q_norm_w': gain(ks[6], (L, MLA_Q_LORA)),
        'mla_w_uq': nrm(ks[7], (L, MLA_Q_LORA, MLA_HEADS * (MLA_D_NOPE + MLA_D_ROPE)), MLA_Q_LORA ** -0.5),
        'mla_kv_norm_w': gain(ks[8], (L, MLA_KV_LORA)),
        'mla_w_ukv': nrm(ks[9], (L, MLA_KV_LORA, MLA_HEADS * (MLA_D_NOPE + MLA_D_V)), MLA_KV_LORA ** -0.5),
        'diff_lambda': nrm(ks[10], (L, 4, DIFF_D_QK), 0.1),
        'diff_subln_w': gain(ks[11], (L, DIFF_D_V)),
        'ssd_conv_w': nrm(ks[12], (L, SSD_CONV, SSD_CONV_DIM), SSD_CONV ** -0.5),
        'ssd_conv_b': nrm(ks[13], (L, SSD_CONV_DIM), 0.02),
        'ssd_dt_bias': dt0 + jnp.log(-jnp.expm1(-dt0)),
        'ssd_a_log': jnp.log(jax.random.uniform(ks[14], (L, SSD_HEADS), jnp.float32, 1.0, 16.0)),
        'ssd_d': gain(ks[15], (L, SSD_HEADS)),
        'ssd_norm_w': gain(ks[17], (L, SSD_D_INNER)),
        'w_out': nrm(ks[18], (L, D_MIX, D_MODEL), D_MIX ** -0.5),
        'norm2_w': gain(ks[19], (L, D_MODEL)),
        'router_w_group': nrm(ks[20], (L, D_MODEL, MOE_GROUPS), D_MODEL ** -0.5),
        'router_b_group': nrm(ks[21], (L, MOE_GROUPS), 0.01),
        'router_w_expert': nrm(ks[22], (L, D_MODEL, N_EXPERTS), D_MODEL ** -0.5),
        'router_b_expert': nrm(ks[23], (L, N_EXPERTS), 0.01),
        'exp_w_gate': nrm(ks[24], (L, N_EXPERTS, D_MODEL, D_EXPERT), D_MODEL ** -0.5),
        'exp_w_up': nrm(ks[25], (L, N_EXPERTS, D_MODEL, D_EXPERT), D_MODEL ** -0.5),
        'exp_w_down': nrm(ks[26], (L, N_EXPERTS, D_EXPERT, D_MODEL), D_EXPERT ** -0.5),
        'final_norm_w': gain(ks[27], (D_MODEL,)),
    }


def reference(x, c, positions, w_ada, b_ada, norm1_w, w_in, mla_q_norm_w, mla_w_uq, mla_kv_norm_w, mla_w_ukv,
              diff_lambda, diff_subln_w, ssd_conv_w, ssd_conv_b, ssd_dt_bias, ssd_a_log, ssd_d, ssd_norm_w,
              w_out, norm2_w, router_w_group, router_b_group, router_w_expert, router_b_expert,
              exp_w_gate, exp_w_up, exp_w_down, final_norm_w):
    cond = jax.nn.silu(c)
    for l in range(DEPTH):
        mod = cond @ w_ada[l] + b_ada[l]
        sh1, sc1, g1, sh2, sc2, g2 = jnp.split(mod, 6, axis=-1)
        h = modulate(rmsnorm(x, norm1_w[l]), sh1, sc1)
        y = hybrid_mixer(h, positions, l, w_in[l], mla_q_norm_w[l], mla_w_uq[l], mla_kv_norm_w[l], mla_w_ukv[l],
                         diff_lambda[l], diff_subln_w[l], ssd_conv_w[l], ssd_conv_b[l], ssd_dt_bias[l],
                         ssd_a_log[l], ssd_d[l], ssd_norm_w[l], w_out[l])
        x = x + g1[:, None, :] * y
        h = modulate(rmsnorm(x, norm2_w[l]), sh2, sc2)
        y = hier_moe(h, router_w_group[l], router_b_group[l], router_w_expert[l], router_b_expert[l],
                     exp_w_gate[l], exp_w_up[l], exp_w_down[l])
        x = x + g2[:, None, :] * y
    return rmsnorm(x, final_norm_w)
```

```python
import functools
import math

import numpy as np
import jax
import jax.numpy as jnp
from jax import lax
from jax.experimental import pallas as pl
from jax.experimental.pallas import tpu as pltpu

F32 = jnp.float32
BF16 = jnp.bfloat16
HIGHEST = lax.Precision.HIGHEST

D_MODEL = 1024
DEPTH = 2
MLA_HEADS = 4
MLA_Q_LORA = 256
MLA_KV_LORA = 128
MLA_D_NOPE = 128
MLA_D_ROPE = 64
MLA_D_V = 128
ROPE_THETA = 10000.0
DIFF_HEADS = 4
DIFF_D_QK = 64
DIFF_D_V = 128
SSD_HEADS = 16
SSD_HEAD_DIM = 64
SSD_D_INNER = 1024
SSD_GROUPS = 4
SSD_STATE = 128
SSD_CONV = 4
SSD_CHUNK = 128
SSD_CONV_DIM = SSD_D_INNER + 2 * SSD_GROUPS * SSD_STATE
MOE_GROUPS = 4
MOE_EXPERTS_PER_GROUP = 8
N_EXPERTS = 32
D_EXPERT = 512
NORM_EPS = 1e-6
SUBLN_EPS = 1e-5

LANES = 128
SUBLANES = 8
MOE_ROWS = 256
VMEM_LIMIT = 48 * 1024 * 1024

_OFF = np.cumsum([0, MLA_Q_LORA, MLA_KV_LORA, MLA_D_ROPE, 512, 512, 512, SSD_D_INNER, SSD_CONV_DIM, SSD_HEADS])


def _cparams(sem):
    return pltpu.CompilerParams(dimension_semantics=sem, vmem_limit_bytes=VMEM_LIMIT)


def _dot(a, b, **kw):
    return jnp.dot(a, b, preferred_element_type=F32, **kw)


def _dot_nt(a, b):
    return lax.dot_general(a, b, (((1,), (1,)), ((), ())), preferred_element_type=F32)


def _rms(x, eps):
    return x * lax.rsqrt(jnp.mean(x * x, axis=-1, keepdims=True) + eps)


def _sigmoid(x):
    return 1.0 / (1.0 + jnp.exp(-x))


def _ada_kernel(c_ref, w_ref, b_ref, o_ref):
    c = c_ref[...]
    cond = c * _sigmoid(c)
    o_ref[...] = _dot(cond, w_ref[...], precision=HIGHEST) + b_ref[...]


def _ada(c_pad, w_ada, b_ada):
    L, D, N = w_ada.shape
    tn = 1024
    return pl.pallas_call(
        _ada_kernel,
        grid=(L, N // tn),
        in_specs=[
            pl.BlockSpec((SUBLANES, D), lambda l, j: (0, 0)),
            pl.BlockSpec((None, D, tn), lambda l, j: (l, 0, j)),
            pl.BlockSpec((None, 1, tn), lambda l, j: (l, 0, j)),
        ],
        out_specs=pl.BlockSpec((None, SUBLANES, tn), lambda l, j: (l, 0, j)),
        out_shape=jax.ShapeDtypeStruct((L, SUBLANES, N), F32),
        compiler_params=_cparams(("parallel", "parallel")),
        name="ada",
    )(c_pad, w_ada, b_ada.reshape(L, 1, N))


def _inproj_kernel(x_ref, nw_ref, sh_ref, sc_ref, wm_ref, wq_ref, wk_ref, wv_ref, wz_ref, wx_ref, wt_ref,
                   om_ref, oq_ref, ok_ref, ov_ref, oz_ref, ox_ref, ot_ref):
    x = x_ref[...]
    h = _rms(x, NORM_EPS) * nw_ref[...] * (1.0 + sc_ref[...]) + sh_ref[...]
    hb = h.astype(BF16)
    for w_ref, o_ref in ((wm_ref, om_ref), (wq_ref, oq_ref), (wk_ref, ok_ref), (wv_ref, ov_ref),
                         (wz_ref, oz_ref), (wx_ref, ox_ref), (wt_ref, ot_ref)):
        o_ref[...] = _dot(hb, w_ref[...]).astype(o_ref.dtype)


def _inproj(xf, nw, sh, sc, ws, seq):
    T, D = xf.shape
    tm = 256
    per_b = seq // tm
    widths = [w.shape[1] for w in ws]
    row = lambda i: (i, 0)
    const = lambda i: (0, 0)
    bat = lambda i: (i // per_b, 0, 0)
    in_specs = [pl.BlockSpec((tm, D), row), pl.BlockSpec((1, D), const),
                pl.BlockSpec((None, 1, D), bat), pl.BlockSpec((None, 1, D), bat)]
    in_specs += [pl.BlockSpec((D, n), const) for n in widths]
    out_dt = [BF16] * 6 + [F32]
    return pl.pallas_call(
        _inproj_kernel,
        grid=(T // tm,),
        in_specs=in_specs,
        out_specs=[pl.BlockSpec((tm, n), row) for n in widths],
        out_shape=[jax.ShapeDtypeStruct((T, n), dt) for n, dt in zip(widths, out_dt)],
        compiler_params=_cparams(("parallel",)),
        name="inproj",
    )(xf, nw, sh, sc, *ws)


def _mla_prep_kernel(a_ref, pos_ref, inv_ref, qnw_ref, kvnw_ref, wq_ref, wkv_ref, q_ref, k_ref, v_ref):
    a = a_ref[...].astype(F32)
    cqn = (_rms(a[:, :256], NORM_EPS) * qnw_ref[...]).astype(BF16)
    ckvn = (_rms(a[:, 256:384], NORM_EPS) * kvnw_ref[...]).astype(BF16)
    ang = pos_ref[...] * inv_ref[...]
    lane = lax.broadcasted_iota(jnp.int32, ang.shape, 1)
    cs = jnp.where(lane < MLA_D_ROPE, jnp.cos(ang), jnp.sin(ang))
    kp = a[:, 384:512] * cs
    krkr = kp + pltpu.roll(kp, MLA_D_ROPE, axis=1)
    scale = (MLA_D_NOPE + MLA_D_ROPE) ** -0.5
    qall = _dot(cqn, wq_ref[...]) * scale
    kvall = _dot(ckvn, wkv_ref[...])
    for h in range(MLA_HEADS):
        o = h * 256
        q_ref[h] = jnp.concatenate([qall[:, o:o + 128], qall[:, o + 128:o + 256] * cs], axis=1).astype(BF16)
        k_ref[h] = jnp.concatenate([kvall[:, o:o + 128], krkr], axis=1).astype(BF16)
        v_ref[h] = kvall[:, o + 128:o + 256].astype(BF16)


def _mla_prep(a, pos_col, inv, qnw, kvnw, wq, wkv):
    T = a.shape[0]
    tm = 512
    row = lambda i: (i, 0)
    const = lambda i: (0, 0)
    hrow = lambda i: (0, i, 0)
    return pl.pallas_call(
        _mla_prep_kernel,
        grid=(T // tm,),
        in_specs=[pl.BlockSpec((tm, 512), row), pl.BlockSpec((tm, 1), row), pl.BlockSpec((1, LANES), const),
                  pl.BlockSpec((1, 256), const), pl.BlockSpec((1, 128), const),
                  pl.BlockSpec((256, 1024), const), pl.BlockSpec((128, 1024), const)],
        out_specs=[pl.BlockSpec((MLA_HEADS, tm, 256), hrow), pl.BlockSpec((MLA_HEADS, tm, 256), hrow),
                   pl.BlockSpec((MLA_HEADS, tm, 128), hrow)],
        out_shape=[jax.ShapeDtypeStruct((MLA_HEADS, T, 256), BF16), jax.ShapeDtypeStruct((MLA_HEADS, T, 256), BF16),
                   jax.ShapeDtypeStruct((MLA_HEADS, T, 128), BF16)],
        compiler_params=_cparams(("parallel",)),
        name="mla_prep",
    )(a, pos_col, inv, qnw, kvnw, wq, wkv)


def _flash_step(s, v, carry):
    m, l, acc = carry
    m_new = jnp.maximum(m, jnp.max(s, axis=-1, keepdims=True))
    alpha = jnp.exp(m - m_new)
    p = jnp.exp(s - m_new)
    l = alpha * l + jnp.sum(p, axis=-1, keepdims=True)
    acc = alpha * acc + _dot(p.astype(BF16), v)
    return m_new, l, acc


def _flash_init(rows, dv):
    return (jnp.full((rows, 1), -jnp.inf, F32), jnp.zeros((rows, 1), F32), jnp.zeros((rows, dv), F32))


def _mla_flash_kernel(q_ref, k_ref, v_ref, o_ref, *, tq):
    qi = pl.program_id(2)
    q = q_ref[...]

    def step(kb, carry, masked):
        start = pl.multiple_of(kb * tq, tq)
        s = _dot_nt(q, k_ref[pl.ds(start, tq), :])
        if masked:
            r = lax.broadcasted_iota(jnp.int32, s.shape, 0)
            c = lax.broadcasted_iota(jnp.int32, s.shape, 1)
            s = jnp.where(r >= c, s, -jnp.inf)
        return _flash_step(s, v_ref[pl.ds(start, tq), :], carry)

    carry = lax.fori_loop(0, qi, lambda kb, cy: step(kb, cy, False), _flash_init(tq, MLA_D_V))
    _, l, acc = step(qi, carry, True)
    o_ref[...] = (acc / l).astype(o_ref.dtype)


def _mla_flash(q, k, v, batch, seq):
    tq = min(512, seq)
    nq = seq // tq
    return pl.pallas_call(
        functools.partial(_mla_flash_kernel, tq=tq),
        grid=(batch, MLA_HEADS, nq),
        in_specs=[pl.BlockSpec((None, tq, 256), lambda b, h, i: (h, b * nq + i, 0)),
                  pl.BlockSpec((None, seq, 256), lambda b, h, i: (h, b, 0)),
                  pl.BlockSpec((None, seq, 128), lambda b, h, i: (h, b, 0))],
        out_specs=pl.BlockSpec((tq, MLA_D_V), lambda b, h, i: (b * nq + i, h)),
        out_shape=jax.ShapeDtypeStruct((batch * seq, MLA_HEADS * MLA_D_V), BF16),
        compiler_params=_cparams(("parallel", "parallel", "arbitrary")),
        name="mla_flash",
    )(q, k, v)


def _diff_flash_kernel(q_ref, k_ref, v_ref, pq_ref, pk_ref, slope_ref, lam_ref, sw_ref, o_ref, *, tq, lam_init):
    qi = pl.program_id(2)
    q = q_ref[...]
    lane = lax.broadcasted_iota(jnp.int32, q.shape, 1)
    zero = jnp.zeros_like(q)
    scale = jnp.asarray(DIFF_D_QK ** -0.5, BF16)
    q2 = jnp.concatenate([jnp.where(lane < DIFF_D_QK, q, zero), jnp.where(lane >= DIFF_D_QK, q, zero)], axis=0) * scale
    pq = pq_ref[...]
    pq2 = jnp.concatenate([pq, pq], axis=0)
    slope = slope_ref[...]

    def step(kb, carry, masked):
        start = pl.multiple_of(kb * tq, tq)
        s = _dot_nt(q2, k_ref[pl.ds(start, tq), :]) - slope * jnp.abs(pq2 - pk_ref[kb])
        if masked:
            r = lax.broadcasted_iota(jnp.int32, s.shape, 0)
            r = jnp.where(r >= tq, r - tq, r)
            c = lax.broadcasted_iota(jnp.int32, s.shape, 1)
            s = jnp.where(r >= c, s, -jnp.inf)
        return _flash_step(s, v_ref[pl.ds(start, tq), :], carry)

    carry = lax.fori_loop(0, qi, lambda kb, cy: step(kb, cy, False), _flash_init(2 * tq, DIFF_D_V))
    _, l, acc = step(qi, carry, True)
    o2 = acc / l
    lf = lam_ref[...]
    lam = (jnp.exp(jnp.sum(lf[0:1] * lf[1:2], keepdims=True))
           - jnp.exp(jnp.sum(lf[2:3] * lf[3:4], keepdims=True)) + lam_init)
    out = o2[:tq] - lam * o2[tq:]
    o_ref[...] = (_rms(out, SUBLN_EPS) * sw_ref[...] * (1.0 - lam_init)).astype(o_ref.dtype)


def _diff_flash(q, k, v, pos_col, pos_row, slopes, lam, sw, batch, seq, lam_init):
    tq = min(256, seq)
    nq = seq // tq
    return pl.pallas_call(
        functools.partial(_diff_flash_kernel, tq=tq, lam_init=lam_init),
        grid=(batch, DIFF_HEADS, nq),
        in_specs=[pl.BlockSpec((tq, 128), lambda b, h, i: (b * nq + i, h)),
                  pl.BlockSpec((seq, 128), lambda b, h, i: (b, h)),
                  pl.BlockSpec((seq, 128), lambda b, h, i: (b, h)),
                  pl.BlockSpec((tq, 1), lambda b, h, i: (b * nq + i, 0)),
                  pl.BlockSpec((None, nq, 1, tq), lambda b, h, i: (b, 0, 0, 0)),
                  pl.BlockSpec((None, 1, 1), lambda b, h, i: (h, 0, 0)),
                  pl.BlockSpec((4, DIFF_D_QK), lambda b, h, i: (0, 0)),
                  pl.BlockSpec((1, DIFF_D_V), lambda b, h, i: (0, 0))],
        out_specs=pl.BlockSpec((tq, DIFF_D_V), lambda b, h, i: (b * nq + i, h)),
        out_shape=jax.ShapeDtypeStruct((batch * seq, DIFF_HEADS * DIFF_D_V), BF16),
        compiler_params=_cparams(("parallel", "parallel", "arbitrary")),
        name="diff_flash",
    )(q, k, v, pos_col, pos_row.reshape(batch, nq, 1, tq), slopes, lam, sw)


def _ssd_kernel(xbc_ref, z_ref, dt_ref, cw_ref, cb_ref, dtb_ref, alog_ref, dsk_ref, nw_ref, e_ref, o_ref,
                ext, state):
    L = SSD_CHUNK
    G, N = SSD_GROUPS, SSD_STATE
    GW = SSD_D_INNER // G

    @pl.when(pl.program_id(1) == 0)
    def _():
        ext[0:SUBLANES, :] = jnp.zeros((SUBLANES, SSD_CONV_DIM), F32)
        state[...] = jnp.zeros(state.shape, F32)

    u = xbc_ref[...].astype(F32)
    ext[SUBLANES:SUBLANES + L, :] = u
    acc = cb_ref[...] + cw_ref[SSD_CONV - 1:SSD_CONV, :] * u
    for k in range(SSD_CONV - 1):
        o = SUBLANES - (SSD_CONV - 1) + k
        acc = acc + cw_ref[k:k + 1, :] * ext[o:o + L, :]
    ext[0:SUBLANES, :] = u[L - SUBLANES:L, :]
    act = acc * _sigmoid(acc)
    xs = act[:, :SSD_D_INNER]
    bm = act[:, SSD_D_INNER:SSD_D_INNER + G * N]
    cm = act[:, SSD_D_INNER + G * N:]

    dtr = dt_ref[...] + dtb_ref[...]
    dtv = jnp.maximum(dtr, 0.0) + jnp.log1p(jnp.exp(-jnp.abs(dtr)))
    adt = dtv * (-jnp.exp(alog_ref[...]))
    r = lax.broadcasted_iota(jnp.int32, (L, L), 0)
    c = lax.broadcasted_iota(jnp.int32, (L, L), 1)
    tril = r >= c
    a_cs = _dot(tril.astype(F32), adt, precision=HIGHEST)
    a_cs_t = a_cs.T
    ea = jnp.exp(a_cs)
    dte = jnp.exp(a_cs[L - 1:L, :] - a_cs)
    stack = jnp.concatenate([dtv, dte, ea], axis=0)
    hi = stack.astype(BF16)
    lo = (stack - hi.astype(F32)).astype(BF16)
    ex = _dot(hi, e_ref[...]) + _dot(lo, e_ref[...])
    dt_e, dte_e, ea_e = ex[:L], ex[L:2 * L], ex[2 * L:]
    xdt = xs * dt_e
    xdt_b = xdt.astype(BF16)
    xw_b = (xdt * dte_e).astype(BF16)
    bb = bm.astype(BF16)
    cbf = cm.astype(BF16)
    lane_g = lax.broadcasted_iota(jnp.int32, (1, GW), 1)
    ys = []
    for g in range(G):
        bg = bb[:, g * N:(g + 1) * N]
        cg = cbf[:, g * N:(g + 1) * N]
        cb = _dot_nt(cg, bg)
        xg = xdt_b[:, g * GW:(g + 1) * GW]
        yd = jnp.zeros((L, GW), F32)
        for rr in range(SSD_HEADS // G):
            h = g * (SSD_HEADS // G) + rr
            seg = a_cs[:, h:h + 1] - a_cs_t[h:h + 1, :]
            dec = jnp.exp(jnp.where(tril, seg, -jnp.inf))
            in_head = (lane_g >= rr * SSD_HEAD_DIM) & (lane_g < (rr + 1) * SSD_HEAD_DIM)
            yd = yd + _dot((cb * dec).astype(BF16), jnp.where(in_head, xg, jnp.zeros_like(xg)))
        st_old = state[g]
        eg = ea_e[:, g * GW:(g + 1) * GW]
        yoff = _dot(cg, st_old.astype(BF16)) * eg
        bg_t = bm[:, g * N:(g + 1) * N].T.astype(BF16)
        state[g] = st_old * eg[L - 1:L, :] + _dot(bg_t, xw_b[:, g * GW:(g + 1) * GW])
        ys.append(yd + yoff)
    y = jnp.concatenate(ys, axis=1) + xs * dsk_ref[...]
    zf = z_ref[...].astype(F32)
    y = y * (zf * _sigmoid(zf))
    y = jnp.concatenate([_rms(y[:, g * GW:(g + 1) * GW], NORM_EPS) for g in range(G)], axis=1)
    o_ref[...] = (y * nw_ref[...]).astype(o_ref.dtype)


def _ssd(xbc, z, dt, cw, cb, dtb, alog, dsk, nw, emat, batch, seq):
    L = SSD_CHUNK
    nc = seq // L
    row = lambda b, c: (b * nc + c, 0)
    const = lambda b, c: (0, 0)
    return pl.pallas_call(
        _ssd_kernel,
        grid=(batch, nc),
        in_specs=[pl.BlockSpec((L, SSD_CONV_DIM), row), pl.BlockSpec((L, SSD_D_INNER), row),
                  pl.BlockSpec((L, LANES), row),
                  pl.BlockSpec((SSD_CONV, SSD_CONV_DIM), const), pl.BlockSpec((1, SSD_CONV_DIM), const),
                  pl.BlockSpec((1, LANES), const), pl.BlockSpec((1, LANES), const),
                  pl.BlockSpec((1, SSD_D_INNER), const), pl.BlockSpec((1, SSD_D_INNER), const),
                  pl.BlockSpec((LANES, SSD_D_INNER), const)],
        out_specs=pl.BlockSpec((L, SSD_D_INNER), row),
        out_shape=jax.ShapeDtypeStruct((batch * seq, SSD_D_INNER), BF16),
        scratch_shapes=[pltpu.VMEM((SUBLANES + L, SSD_CONV_DIM), F32),
                        pltpu.VMEM((SSD_GROUPS, SSD_STATE, SSD_D_INNER // SSD_GROUPS), F32)],
        compiler_params=_cparams(("arbitrary", "arbitrary")),
        name="ssd",
    )(xbc, z, dt, cw, cb, dtb, alog, dsk, nw, emat)


def _outproj_router_kernel(x_ref, ym_ref, yd_ref, ys_ref, g1_ref, sh_ref, sc_ref, nw_ref, wm_ref, wd_ref, ws_ref,
                           wr_ref, br_ref, xo_ref, h_ref, route_ref, cnt_ref, carry):
    @pl.when(pl.program_id(0) == 0)
    def _():
        carry[...] = jnp.zeros(carry.shape, F32)

    y = _dot(ym_ref[...], wm_ref[...]) + _dot(yd_ref[...], wd_ref[...]) + _dot(ys_ref[...], ws_ref[...])
    xn = x_ref[...] + g1_ref[...] * y
    xo_ref[...] = xn
    h = _rms(xn, NORM_EPS) * nw_ref[...] * (1.0 + sc_ref[...]) + sh_ref[...]
    h_ref[...] = h
    logits = _dot(h, wr_ref[...], precision=HIGHEST) + br_ref[...]
    tm = logits.shape[0]
    lane = lax.broadcasted_iota(jnp.int32, logits.shape, 1)
    big = jnp.int32(1 << 20)
    neg = -jnp.inf
    lg = jnp.where((lane >= N_EXPERTS) & (lane < N_EXPERTS + MOE_GROUPS), logits, neg)
    mg = jnp.max(lg, axis=-1, keepdims=True)
    pg_top = 1.0 / jnp.sum(jnp.exp(lg - mg), axis=-1, keepdims=True)
    gsel = jnp.min(jnp.where(lg == mg, lane, big), axis=-1, keepdims=True) - N_EXPERTS
    le = jnp.where((lane < N_EXPERTS) & ((lane // MOE_EXPERTS_PER_GROUP) == gsel), logits, neg)
    m1 = jnp.max(le, axis=-1, keepdims=True)
    e1 = jnp.min(jnp.where(le == m1, lane, big), axis=-1, keepdims=True)
    le2 = jnp.where(lane == e1, neg, le)
    m2 = jnp.max(le2, axis=-1, keepdims=True)
    e2 = jnp.min(jnp.where(le2 == m2, lane, big), axis=-1, keepdims=True)
    rr = jnp.exp(m2 - m1)
    gate1 = pg_top / (1.0 + rr)
    gate2 = pg_top * rr / (1.0 + rr)
    onehot = jnp.where((lane == e1) | (lane == e2), 1.0, 0.0)
    ri = lax.broadcasted_iota(jnp.int32, (tm, tm), 0)
    ci = lax.broadcasted_iota(jnp.int32, (tm, tm), 1)
    before = jnp.where(ri > ci, 1.0, 0.0).astype(BF16)
    prefix = _dot(before, onehot.astype(BF16)) + carry[...]
    rank1 = jnp.sum(jnp.where(lane == e1, prefix, 0.0), axis=-1, keepdims=True)
    rank2 = jnp.sum(jnp.where(lane == e2, prefix, 0.0), axis=-1, keepdims=True)
    carry[...] = carry[...] + jnp.sum(onehot, axis=0, keepdims=True)
    cnt_ref[...] = carry[...]
    cols = (e1.astype(F32), e2.astype(F32), rank1, rank2, gate1, gate2)
    route = jnp.zeros(logits.shape, F32)
    for i, col in enumerate(cols):
        route = jnp.where(lane == i, col, route)
    route_ref[...] = route


def _outproj_router(xf, ym, yd, ys, g1, sh2, sc2, n2w, wom, wod, wos, wr, br, seq):
    T, D = xf.shape
    tm = 256
    per_b = seq // tm
    row = lambda i: (i, 0)
    const = lambda i: (0, 0)
    bat = lambda i: (i // per_b, 0, 0)
    return pl.pallas_call(
        _outproj_router_kernel,
        grid=(T // tm,),
        in_specs=[pl.BlockSpec((tm, D), row), pl.BlockSpec((tm, 512), row), pl.BlockSpec((tm, 512), row),
                  pl.BlockSpec((tm, SSD_D_INNER), row),
                  pl.BlockSpec((None, 1, D), bat), pl.BlockSpec((None, 1, D), bat), pl.BlockSpec((None, 1, D), bat),
                  pl.BlockSpec((1, D), const),
                  pl.BlockSpec((512, D), const), pl.BlockSpec((512, D), const), pl.BlockSpec((SSD_D_INNER, D), const),
                  pl.BlockSpec((D, LANES), const), pl.BlockSpec((1, LANES), const)],
        out_specs=[pl.BlockSpec((tm, D), row), pl.BlockSpec((tm, D), row), pl.BlockSpec((tm, LANES), row),
                   pl.BlockSpec((1, LANES), const)],
        out_shape=[jax.ShapeDtypeStruct((T, D), F32), jax.ShapeDtypeStruct((T, D), F32),
                   jax.ShapeDtypeStruct((T, LANES), F32), jax.ShapeDtypeStruct((1, LANES), F32)],
        scratch_shapes=[pltpu.VMEM((1, LANES), F32)],
        compiler_params=_cparams(("arbitrary",)),
        name="outproj_router",
    )(xf, ym, yd, ys, g1, sh2, sc2, n2w, wom, wod, wos, wr, br)


def _row_copy(src, dst, i_src, i_dst, sem):
    return pltpu.make_async_copy(src.at[pl.ds(i_src, 1)], dst.at[pl.ds(i_dst, 1)], sem)


def _dispatch_kernel(dest_ref, h_ref, xb_in_ref, xb_ref, sem):
    del xb_in_ref
    tm = h_ref.shape[0]

    def issue(r, _):
        for k in range(2):
            _row_copy(h_ref, xb_ref, r, dest_ref[2 * r + k], sem).start()
        return 0

    lax.fori_loop(0, tm, issue, 0)

    def drain(r, _):
        for k in range(2):
            _row_copy(h_ref, xb_ref, 0, 0, sem).wait()
        return 0

    lax.fori_loop(0, tm, drain, 0)


def _dispatch(dest, h, xb_zero):
    T, D = h.shape
    tm = 256
    return pl.pallas_call(
        _dispatch_kernel,
        grid=(T // tm,),
        in_specs=[pl.BlockSpec((2 * tm,), lambda i: (i,), memory_space=pltpu.SMEM),
                  pl.BlockSpec((tm, D), lambda i: (i, 0)),
                  pl.BlockSpec(memory_space=pl.ANY)],
        out_specs=pl.BlockSpec(memory_space=pl.ANY),
        out_shape=jax.ShapeDtypeStruct(xb_zero.shape, F32),
        scratch_shapes=[pltpu.SemaphoreType.DMA(())],
        input_output_aliases={2: 0},
        compiler_params=_cparams(("arbitrary",)),
        name="moe_dispatch",
    )(dest, h, xb_zero)


def _expert_kernel(bexp_ref, nused_ref, x_ref, wg_ref, wu_ref, wd_ref, o_ref, wg_s, wu_s, wd_s):
    j = pl.program_id(0)
    used = j < nused_ref[0]
    new_expert = (j == 0) | (bexp_ref[j] != bexp_ref[jnp.maximum(j - 1, 0)])

    @pl.when(used & new_expert)
    def _():
        wg_s[...] = wg_ref[...].astype(BF16)
        wu_s[...] = wu_ref[...].astype(BF16)
        wd_s[...] = wd_ref[...].astype(BF16)

    @pl.when(used)
    def _():
        x = x_ref[...].astype(BF16)
        g = _dot(x, wg_s[...])
        u = _dot(x, wu_s[...])
        o_ref[...] = _dot((g * _sigmoid(g) * u).astype(BF16), wd_s[...])

    @pl.when(jnp.logical_not(used))
    def _():
        o_ref[...] = jnp.zeros(o_ref.shape, F32)


def _experts(bexp, nused, xb, w_gate, w_up, w_down, layer):
    cap, D = xb.shape
    nb = cap // MOE_ROWS
    F = D_EXPERT
    last = lambda j, be, nu: jnp.minimum(j, nu[0] - 1)
    return pl.pallas_call(
        _expert_kernel,
        grid_spec=pltpu.PrefetchScalarGridSpec(
            num_scalar_prefetch=2,
            grid=(nb,),
            in_specs=[pl.BlockSpec((MOE_ROWS, D), lambda j, be, nu: (last(j, be, nu), 0)),
                      pl.BlockSpec((None, None, D, F), lambda j, be, nu: (layer, be[last(j, be, nu)], 0, 0)),
                      pl.BlockSpec((None, None, D, F), lambda j, be, nu: (layer, be[last(j, be, nu)], 0, 0)),
                      pl.BlockSpec((None, None, F, D), lambda j, be, nu: (layer, be[last(j, be, nu)], 0, 0))],
            out_specs=pl.BlockSpec((MOE_ROWS, D), lambda j, be, nu: (j, 0)),
            scratch_shapes=[pltpu.VMEM((D, F), BF16), pltpu.VMEM((D, F), BF16), pltpu.VMEM((F, D), BF16)],
        ),
        out_shape=jax.ShapeDtypeStruct((cap, D), F32),
        compiler_params=_cparams(("arbitrary",)),
        name="moe_experts",
    )(bexp, nused, xb, w_gate, w_up, w_down)


def _combine_kernel(dest_ref, x_ref, route_ref, g2_ref, fnw_ref, yb_ref, o_ref, gbuf, sem, *, final):
    tm = x_ref.shape[0]

    def issue(r, _):
        for k in range(2):
            _row_copy(yb_ref, gbuf.at[k], dest_ref[2 * r + k], r, sem).start()
        return 0

    lax.fori_loop(0, tm, issue, 0)

    def drain(r, _):
        for k in range(2):
            _row_copy(yb_ref, gbuf.at[k], 0, 0, sem).wait()
        return 0

    lax.fori_loop(0, tm, drain, 0)
    route = route_ref[...]
    y = route[:, 4:5] * gbuf[0] + route[:, 5:6] * gbuf[1]
    xo = x_ref[...] + g2_ref[...] * y
    if final:
        xo = _rms(xo, NORM_EPS) * fnw_ref[...]
    o_ref[...] = xo


def _combine(dest, xf, route, g2, fnw, yb, seq, final):
    T, D = xf.shape
    tm = 256
    per_b = seq // tm
    return pl.pallas_call(
        functools.partial(_combine_kernel, final=final),
        grid=(T // tm,),
        in_specs=[pl.BlockSpec((2 * tm,), lambda i: (i,), memory_space=pltpu.SMEM),
                  pl.BlockSpec((tm, D), lambda i: (i, 0)),
                  pl.BlockSpec((tm, LANES), lambda i: (i, 0)),
                  pl.BlockSpec((None, 1, D), lambda i: (i // per_b, 0, 0)),
                  pl.BlockSpec((1, D), lambda i: (0, 0)),
                  pl.BlockSpec(memory_space=pl.ANY)],
        out_specs=pl.BlockSpec((tm, D), lambda i: (i, 0)),
        out_shape=jax.ShapeDtypeStruct((T, D), F32),
        scratch_shapes=[pltpu.VMEM((2, tm, D), F32), pltpu.SemaphoreType.DMA(())],
        compiler_params=_cparams(("arbitrary",)),
        name="moe_combine",
    )(dest, xf, route, g2, fnw, yb)


def _rot_cols(w):
    half = w.shape[1] // 2
    return jnp.concatenate([-w[:, half:], w[:, :half]], axis=1)


def _pad_lanes(a, n=LANES):
    return jnp.pad(a, [(0, 0)] * (a.ndim - 1) + [(0, n - a.shape[-1])])


def kernel(x, c, positions, w_ada, b_ada, norm1_w, w_in, mla_q_norm_w, mla_w_uq, mla_kv_norm_w, mla_w_ukv, diff_lambda, diff_subln_w, ssd_conv_w, ssd_conv_b, ssd_dt_bias, ssd_a_log, ssd_d, ssd_norm_w, w_out, norm2_w, router_w_group, router_b_group, router_w_expert, router_b_expert, exp_w_gate, exp_w_up, exp_w_down, final_norm_w):
    B, S, D = x.shape
    T = B * S
    xf = x.reshape(T, D)
    mod = _ada(jnp.pad(c, ((0, SUBLANES - B), (0, 0))), w_ada, b_ada)[:, :B]

    pos_f = positions.astype(F32)
    pos_col = pos_f.reshape(T, 1)
    inv = 1.0 / (ROPE_THETA ** (jnp.arange(0, MLA_D_ROPE, 2, dtype=F32) / MLA_D_ROPE))
    inv = jnp.tile(inv, 4).reshape(1, LANES)
    slopes = jnp.exp2(-8.0 / DIFF_HEADS * jnp.arange(1, DIFF_HEADS + 1, dtype=F32)).reshape(DIFF_HEADS, 1, 1)
    head_of_col = np.arange(SSD_D_INNER) // SSD_HEAD_DIM
    emat = jnp.asarray(np.arange(LANES)[:, None] == head_of_col[None, :], BF16)

    cap = T * 2 + N_EXPERTS * MOE_ROWS
    nb = cap // MOE_ROWS
    xb_zero = jnp.zeros((cap, D), F32)
    fnw = final_norm_w.reshape(1, D)

    for l in range(DEPTH):
        sh1, sc1, g1, sh2, sc2, g2 = [mod[l, :, i * D:(i + 1) * D].reshape(B, 1, D) for i in range(6)]
        w = w_in[l]
        ws = [jnp.concatenate([w[:, :_OFF[3]], _rot_cols(w[:, _OFF[2]:_OFF[3]])], axis=1)]
        ws += [w[:, _OFF[i]:_OFF[i + 1]] for i in range(3, 8)]
        ws.append(_pad_lanes(w[:, _OFF[8]:_OFF[9]]))
        ws = [a.astype(BF16) for a in ws]
        a_mla, dq, dk, dv, z, xbc, dt = _inproj(xf, norm1_w[l].reshape(1, D), sh1, sc1, ws, S)

        wq = mla_w_uq[l].reshape(MLA_Q_LORA, MLA_HEADS, MLA_D_NOPE + MLA_D_ROPE)
        wq_pe = wq[:, :, MLA_D_NOPE:]
        wq_rot = jnp.concatenate([-wq_pe[:, :, MLA_D_ROPE // 2:], wq_pe[:, :, :MLA_D_ROPE // 2]], axis=2)
        wq_cat = jnp.concatenate([wq, wq_rot], axis=2).reshape(MLA_Q_LORA, MLA_HEADS * 256).astype(BF16)
        q, k, v = _mla_prep(a_mla, pos_col, inv, mla_q_norm_w[l].reshape(1, -1), mla_kv_norm_w[l].reshape(1, -1),
                            wq_cat, mla_w_ukv[l].astype(BF16))
        y_mla = _mla_flash(q, k, v, B, S)

        lam_init = 0.8 - 0.6 * math.exp(-0.3 * l)
        y_diff = _diff_flash(dq, dk, dv, pos_col, pos_f, slopes, diff_lambda[l], diff_subln_w[l].reshape(1, -1),
                             B, S, lam_init)

        y_ssd = _ssd(xbc, z, dt, ssd_conv_w[l], ssd_conv_b[l].reshape(1, -1),
                     _pad_lanes(ssd_dt_bias[l].reshape(1, -1)), _pad_lanes(ssd_a_log[l].reshape(1, -1)),
                     jnp.repeat(ssd_d[l], SSD_HEAD_DIM).reshape(1, -1), ssd_norm_w[l].reshape(1, -1), emat, B, S)

        wo = w_out[l].astype(BF16)
        wr = _pad_lanes(jnp.concatenate([router_w_expert[l], router_w_group[l]], axis=1))
        br = _pad_lanes(jnp.concatenate([router_b_expert[l], router_b_group[l]]).reshape(1, -1))
        x_mid, h2, route, cnt = _outproj_router(xf, y_mla, y_diff, y_ssd, g1, sh2, sc2, norm2_w[l].reshape(1, D),
                                                wo[:512], wo[512:1024], wo[1024:], wr, br, S)

        counts = cnt[0, :N_EXPERTS].astype(jnp.int32)
        pcounts = (counts + MOE_ROWS - 1) // MOE_ROWS * MOE_ROWS
        pend = jnp.cumsum(pcounts)
        pstart = pend - pcounts
        eid = route[:, 0:2].astype(jnp.int32)
        dest = (pstart[eid] + route[:, 2:4].astype(jnp.int32)).reshape(-1)
        bexp = jnp.clip(jnp.searchsorted(pend, jnp.arange(nb, dtype=jnp.int32) * MOE_ROWS, side='right'),
                        0, N_EXPERTS - 1).astype(jnp.int32)
        nused = (pend[-1:] // MOE_ROWS).astype(jnp.int32)

        xb = _dispatch(dest, h2, xb_zero)
        yb = _experts(bexp, nused, xb, exp_w_gate, exp_w_up, exp_w_down, l)
        xf = _combine(dest, x_mid, route, g2, fnw, yb, S, final=(l == DEPTH - 1))
    return xf.reshape(B, S, D)
```

```python
import functools
import math

import numpy as np
import jax
import jax.numpy as jnp
from jax import lax
from jax.experimental import pallas as pl
from jax.experimental.pallas import tpu as pltpu

F32 = jnp.float32
BF16 = jnp.bfloat16
HIGHEST = lax.Precision.HIGHEST

D_MODEL = 1024
DEPTH = 2
MLA_HEADS = 4
MLA_Q_LORA = 256
MLA_KV_LORA = 128
MLA_D_NOPE = 128
MLA_D_ROPE = 64
MLA_D_V = 128
ROPE_THETA = 10000.0
DIFF_HEADS = 4
DIFF_D_QK = 64
DIFF_D_V = 128
SSD_HEADS = 16
SSD_HEAD_DIM = 64
SSD_D_INNER = 1024
SSD_GROUPS = 4
SSD_STATE = 128
SSD_CONV = 4
SSD_CHUNK = 128
SSD_CONV_DIM = SSD_D_INNER + 2 * SSD_GROUPS * SSD_STATE
MOE_GROUPS = 4
MOE_EXPERTS_PER_GROUP = 8
N_EXPERTS = 32
D_EXPERT = 512
NORM_EPS = 1e-6
SUBLN_EPS = 1e-5

LANES = 128
SUBLANES = 8
MOE_ROWS = 256
FLASH_TQ = 512
LOG2E = math.log2(math.e)
VMEM_LIMIT = 48 * 1024 * 1024

_OFF = np.cumsum([0, MLA_Q_LORA, MLA_KV_LORA, MLA_D_ROPE, 512, 512, 512, SSD_D_INNER, SSD_CONV_DIM, SSD_HEADS])


def _cparams(sem):
    return pltpu.CompilerParams(dimension_semantics=sem, vmem_limit_bytes=VMEM_LIMIT)


def _dot(a, b, **kw):
    return jnp.dot(a, b, preferred_element_type=F32, **kw)


def _dot_nt(a, b):
    return lax.dot_general(a, b, (((1,), (1,)), ((), ())), preferred_element_type=F32)


def _rms(x, eps):
    return x * lax.rsqrt(jnp.mean(x * x, axis=-1, keepdims=True) + eps)


def _sigmoid(x):
    return 1.0 / (1.0 + jnp.exp(-x))


def _ada_kernel(c_ref, w_ref, b_ref, o_ref):
    c = c_ref[...]
    cond = c * _sigmoid(c)
    o_ref[...] = _dot(cond, w_ref[...], precision=HIGHEST) + b_ref[...]


def _ada(c_pad, w_ada, b_ada):
    L, D, N = w_ada.shape
    tn = 1024
    return pl.pallas_call(
        _ada_kernel,
        grid=(L, N // tn),
        in_specs=[
            pl.BlockSpec((SUBLANES, D), lambda l, j: (0, 0)),
            pl.BlockSpec((None, D, tn), lambda l, j: (l, 0, j)),
            pl.BlockSpec((None, 1, tn), lambda l, j: (l, 0, j)),
        ],
        out_specs=pl.BlockSpec((None, SUBLANES, tn), lambda l, j: (l, 0, j)),
        out_shape=jax.ShapeDtypeStruct((L, SUBLANES, N), F32),
        compiler_params=_cparams(("parallel", "parallel")),
        name="ada",
    )(c_pad, w_ada, b_ada.reshape(L, 1, N))


def _inproj_kernel(x_ref, nw_ref, sh_ref, sc_ref, wm_ref, wq_ref, wk_ref, wv_ref, wz_ref, wx_ref, wt_ref,
                   om_ref, oq_ref, ok_ref, ov_ref, oz_ref, ox_ref, ot_ref):
    x = x_ref[...]
    h = _rms(x, NORM_EPS) * nw_ref[...] * (1.0 + sc_ref[...]) + sh_ref[...]
    hb = h.astype(BF16)
    for w_ref, o_ref in ((wm_ref, om_ref), (wq_ref, oq_ref), (wk_ref, ok_ref), (wv_ref, ov_ref),
                         (wz_ref, oz_ref), (wx_ref, ox_ref), (wt_ref, ot_ref)):
        o_ref[...] = _dot(hb, w_ref[...]).astype(o_ref.dtype)


def _inproj(xf, nw, sh, sc, ws, seq):
    T, D = xf.shape
    tm = 256
    per_b = seq // tm
    widths = [w.shape[1] for w in ws]
    row = lambda i: (i, 0)
    const = lambda i: (0, 0)
    bat = lambda i: (i // per_b, 0, 0)
    in_specs = [pl.BlockSpec((tm, D), row), pl.BlockSpec((1, D), const),
                pl.BlockSpec((None, 1, D), bat), pl.BlockSpec((None, 1, D), bat)]
    in_specs += [pl.BlockSpec((D, n), const) for n in widths]
    out_dt = [BF16] * 6 + [F32]
    return pl.pallas_call(
        _inproj_kernel,
        grid=(T // tm,),
        in_specs=in_specs,
        out_specs=[pl.BlockSpec((tm, n), row) for n in widths],
        out_shape=[jax.ShapeDtypeStruct((T, n), dt) for n, dt in zip(widths, out_dt)],
        compiler_params=_cparams(("parallel",)),
        name="inproj",
    )(xf, nw, sh, sc, *ws)


def _mla_prep_kernel(a_ref, pos_ref, inv_ref, qnw_ref, kvnw_ref, wq_ref, wkv_ref, q_ref, k_ref, v_ref):
    a = a_ref[...].astype(F32)
    cqn = (_rms(a[:, :256], NORM_EPS) * qnw_ref[...]).astype(BF16)
    ckvn = (_rms(a[:, 256:384], NORM_EPS) * kvnw_ref[...]).astype(BF16)
    ang = pos_ref[...] * inv_ref[...]
    lane = lax.broadcasted_iota(jnp.int32, ang.shape, 1)
    cs = jnp.where(lane < MLA_D_ROPE, jnp.cos(ang), jnp.sin(ang))
    kp = a[:, 384:512] * cs
    krkr = kp + pltpu.roll(kp, MLA_D_ROPE, axis=1)
    scale = (MLA_D_NOPE + MLA_D_ROPE) ** -0.5 * LOG2E
    qall = _dot(cqn, wq_ref[...]) * scale
    kvall = _dot(ckvn, wkv_ref[...])
    for h in range(MLA_HEADS):
        o = h * 256
        q_ref[h] = jnp.concatenate([qall[:, o:o + 128], qall[:, o + 128:o + 256] * cs], axis=1).astype(BF16)
        k_ref[h] = jnp.concatenate([kvall[:, o:o + 128], krkr], axis=1).astype(BF16)
        v_ref[h] = kvall[:, o + 128:o + 256].astype(BF16)


def _mla_prep(a, pos_col, inv, qnw, kvnw, wq, wkv):
    T = a.shape[0]
    tm = 512
    row = lambda i: (i, 0)
    const = lambda i: (0, 0)
    hrow = lambda i: (0, i, 0)
    return pl.pallas_call(
        _mla_prep_kernel,
        grid=(T // tm,),
        in_specs=[pl.BlockSpec((tm, 512), row), pl.BlockSpec((tm, 1), row), pl.BlockSpec((1, LANES), const),
                  pl.BlockSpec((1, 256), const), pl.BlockSpec((1, 128), const),
                  pl.BlockSpec((256, 1024), const), pl.BlockSpec((128, 1024), const)],
        out_specs=[pl.BlockSpec((MLA_HEADS, tm, 256), hrow), pl.BlockSpec((MLA_HEADS, tm, 256), hrow),
                   pl.BlockSpec((MLA_HEADS, tm, 128), hrow)],
        out_shape=[jax.ShapeDtypeStruct((MLA_HEADS, T, 256), BF16), jax.ShapeDtypeStruct((MLA_HEADS, T, 256), BF16),
                   jax.ShapeDtypeStruct((MLA_HEADS, T, 128), BF16)],
        compiler_params=_cparams(("parallel",)),
        name="mla_prep",
    )(a, pos_col, inv, qnw, kvnw, wq, wkv)


FLASH_STRIP = LANES


def _flash_init(m_s, l_s, acc_s):
    m_s[...] = jnp.full(m_s.shape, -jnp.inf, F32)
    l_s[...] = jnp.zeros(l_s.shape, F32)
    acc_s[...] = jnp.zeros(acc_s.shape, F32)


def _flash_update(strip_fn, v_t, m_s, l_s, acc_s):
    ps, alphas = [], []
    for j in range(m_s.shape[1] // FLASH_STRIP):
        sl = slice(j * FLASH_STRIP, (j + 1) * FLASH_STRIP)
        t = strip_fn(j)
        m_old = m_s[:, sl]
        m_new = jnp.maximum(m_old, jnp.max(t, axis=0, keepdims=True))
        alpha = jnp.exp2(m_old - m_new)
        p = jnp.exp2(t - m_new)
        l_s[:, sl] = alpha * l_s[:, sl] + jnp.sum(p, axis=0, keepdims=True)
        m_s[:, sl] = m_new
        ps.append(p.astype(BF16))
        alphas.append(alpha)
    acc_s[...] = jnp.concatenate(alphas, axis=1) * acc_s[...] + _dot(v_t, jnp.concatenate(ps, axis=1))


def _causal_strip(t, j, col0):
    r = lax.broadcasted_iota(jnp.int32, t.shape, 0)
    c = lax.broadcasted_iota(jnp.int32, t.shape, 1) + (j * FLASH_STRIP - col0)
    return jnp.where(r <= c, t, -jnp.inf)


def _mla_flash_kernel(qt_ref, k_ref, vt_ref, o_ref, m_s, l_s, acc_s, *, tq):
    qi = pl.program_id(2)
    _flash_init(m_s, l_s, acc_s)
    q_t = qt_ref[...]

    def block(kb, masked):
        start = pl.multiple_of(kb * tq, tq)
        s_t = _dot(k_ref[pl.ds(start, tq), :], q_t)

        def strip(j):
            t = s_t[:, j * FLASH_STRIP:(j + 1) * FLASH_STRIP]
            return _causal_strip(t, j, 0) if masked else t

        _flash_update(strip, vt_ref[kb], m_s, l_s, acc_s)

    def body(kb, _):
        block(kb, False)
        return 0

    lax.fori_loop(0, qi, body, 0)
    block(qi, True)
    o_ref[...] = (acc_s[...] * (1.0 / l_s[...])).T.astype(o_ref.dtype)


def _mla_flash(q_t, k, v_t, batch, seq, tq):
    nq = seq // tq
    return pl.pallas_call(
        functools.partial(_mla_flash_kernel, tq=tq),
        grid=(batch, MLA_HEADS, nq),
        in_specs=[pl.BlockSpec((None, 256, tq), lambda b, h, i: (h, 0, b * nq + i)),
                  pl.BlockSpec((None, seq, 256), lambda b, h, i: (h, b, 0)),
                  pl.BlockSpec((None, nq, MLA_D_V, tq), lambda b, h, i: (h, b, 0, 0))],
        out_specs=pl.BlockSpec((tq, MLA_D_V), lambda b, h, i: (b * nq + i, h)),
        out_shape=jax.ShapeDtypeStruct((batch * seq, MLA_HEADS * MLA_D_V), BF16),
        scratch_shapes=[pltpu.VMEM((1, tq), F32), pltpu.VMEM((1, tq), F32), pltpu.VMEM((MLA_D_V, tq), F32)],
        compiler_params=_cparams(("parallel", "parallel", "arbitrary")),
        name="mla_flash",
    )(q_t, k, v_t)


def _diff_flash_kernel(qt_ref, k_ref, vt_ref, pq_ref, pk_ref, slope_ref, lam_ref, sw_ref, o_ref, m_s, l_s, acc_s,
                       *, tq, lam_init):
    qi = pl.program_id(2)
    _flash_init(m_s, l_s, acc_s)
    q_t = qt_ref[...]
    row = lax.broadcasted_iota(jnp.int32, q_t.shape, 0)
    zero = jnp.zeros_like(q_t)
    q2_t = jnp.concatenate([jnp.where(row < DIFF_D_QK, q_t, zero), jnp.where(row >= DIFF_D_QK, q_t, zero)], axis=1)
    pq = pq_ref[...]
    slope = slope_ref[...]
    spm = tq // FLASH_STRIP

    def block(kb, masked):
        start = pl.multiple_of(kb * tq, tq)
        s_t = _dot(k_ref[pl.ds(start, tq), :], q2_t)
        bias = slope * jnp.abs(pk_ref[kb] - pq)

        def strip(j):
            jq = j % spm
            t = s_t[:, j * FLASH_STRIP:(j + 1) * FLASH_STRIP] - bias[:, jq * FLASH_STRIP:(jq + 1) * FLASH_STRIP]
            return _causal_strip(t, jq, 0) if masked else t

        _flash_update(strip, vt_ref[kb], m_s, l_s, acc_s)

    def body(kb, _):
        block(kb, False)
        return 0

    lax.fori_loop(0, qi, body, 0)
    block(qi, True)
    o2 = acc_s[...] * (1.0 / l_s[...])
    lf = lam_ref[...]
    lam = (jnp.exp(jnp.sum(lf[0:1] * lf[1:2], keepdims=True))
           - jnp.exp(jnp.sum(lf[2:3] * lf[3:4], keepdims=True)) + lam_init)
    out = o2[:, :tq] - lam * o2[:, tq:]
    out = out * lax.rsqrt(jnp.mean(out * out, axis=0, keepdims=True) + SUBLN_EPS) * sw_ref[...] * (1.0 - lam_init)
    o_ref[...] = out.T.astype(o_ref.dtype)


def _diff_flash(q_t, k, v_t, pos_row, pos_kcol, slopes, lam, sw_col, batch, seq, tq, lam_init):
    nq = seq // tq
    return pl.pallas_call(
        functools.partial(_diff_flash_kernel, tq=tq, lam_init=lam_init),
        grid=(batch, DIFF_HEADS, nq),
        in_specs=[pl.BlockSpec((128, tq), lambda b, h, i: (h, b * nq + i)),
                  pl.BlockSpec((seq, 128), lambda b, h, i: (b, h)),
                  pl.BlockSpec((nq, DIFF_D_V, tq), lambda b, h, i: (b, h, 0)),
                  pl.BlockSpec((None, 1, tq), lambda b, h, i: (b, 0, i)),
                  pl.BlockSpec((nq, tq, 1), lambda b, h, i: (b, 0, 0)),
                  pl.BlockSpec((None, 1, 1), lambda b, h, i: (h, 0, 0)),
                  pl.BlockSpec((4, DIFF_D_QK), lambda b, h, i: (0, 0)),
                  pl.BlockSpec((DIFF_D_V, 1), lambda b, h, i: (0, 0))],
        out_specs=pl.BlockSpec((tq, DIFF_D_V), lambda b, h, i: (b * nq + i, h)),
        out_shape=jax.ShapeDtypeStruct((batch * seq, DIFF_HEADS * DIFF_D_V), BF16),
        scratch_shapes=[pltpu.VMEM((1, 2 * tq), F32), pltpu.VMEM((1, 2 * tq), F32),
                        pltpu.VMEM((DIFF_D_V, 2 * tq), F32)],
        compiler_params=_cparams(("parallel", "parallel", "arbitrary")),
        name="diff_flash",
    )(q_t, k, v_t, pos_row, pos_kcol, slopes, lam, sw_col)


def _ssd_kernel(xbc_ref, z_ref, dt_ref, cw_ref, cb_ref, dtb_ref, alog_ref, dsk_ref, nw_ref, e_ref, o_ref,
                ext, state):
    L = SSD_CHUNK
    G, N = SSD_GROUPS, SSD_STATE
    GW = SSD_D_INNER // G

    @pl.when(pl.program_id(1) == 0)
    def _():
        ext[0:SUBLANES, :] = jnp.zeros((SUBLANES, SSD_CONV_DIM), F32)
        state[...] = jnp.zeros(state.shape, F32)

    u = xbc_ref[...].astype(F32)
    ext[SUBLANES:SUBLANES + L, :] = u
    acc = cb_ref[...] + cw_ref[SSD_CONV - 1:SSD_CONV, :] * u
    for k in range(SSD_CONV - 1):
        o = SUBLANES - (SSD_CONV - 1) + k
        acc = acc + cw_ref[k:k + 1, :] * ext[o:o + L, :]
    ext[0:SUBLANES, :] = u[L - SUBLANES:L, :]
    act = acc * _sigmoid(acc)
    xs = act[:, :SSD_D_INNER]
    bm = act[:, SSD_D_INNER:SSD_D_INNER + G * N]
    cm = act[:, SSD_D_INNER + G * N:]

    dtr = dt_ref[...] + dtb_ref[...]
    dtv = jnp.maximum(dtr, 0.0) + jnp.log1p(jnp.exp(-jnp.abs(dtr)))
    adt = dtv * (-jnp.exp(alog_ref[...]))
    r = lax.broadcasted_iota(jnp.int32, (L, L), 0)
    c = lax.broadcasted_iota(jnp.int32, (L, L), 1)
    tril = r >= c
    a_cs = _dot(tril.astype(F32), adt, precision=HIGHEST)
    a_cs_t = a_cs.T
    ea = jnp.exp(a_cs)
    dte = jnp.exp(a_cs[L - 1:L, :] - a_cs)
    stack = jnp.concatenate([dtv, dte, ea], axis=0)
    hi = stack.astype(BF16)
    lo = (stack - hi.astype(F32)).astype(BF16)
    ex = _dot(hi, e_ref[...]) + _dot(lo, e_ref[...])
    dt_e, dte_e, ea_e = ex[:L], ex[L:2 * L], ex[2 * L:]
    xdt = xs * dt_e
    xdt_b = xdt.astype(BF16)
    xw_b = (xdt * dte_e).astype(BF16)
    bb = bm.astype(BF16)
    cbf = cm.astype(BF16)
    lane_g = lax.broadcasted_iota(jnp.int32, (1, GW), 1)
    ys = []
    for g in range(G):
        bg = bb[:, g * N:(g + 1) * N]
        cg = cbf[:, g * N:(g + 1) * N]
        cb = _dot_nt(cg, bg)
        xg = xdt_b[:, g * GW:(g + 1) * GW]
        yd = jnp.zeros((L, GW), F32)
        for rr in range(SSD_HEADS // G):
            h = g * (SSD_HEADS // G) + rr
            seg = a_cs[:, h:h + 1] - a_cs_t[h:h + 1, :]
            dec = jnp.exp(jnp.where(tril, seg, -jnp.inf))
            in_head = (lane_g >= rr * SSD_HEAD_DIM) & (lane_g < (rr + 1) * SSD_HEAD_DIM)
            yd = yd + _dot((cb * dec).astype(BF16), jnp.where(in_head, xg, jnp.zeros_like(xg)))
        st_old = state[g]
        eg = ea_e[:, g * GW:(g + 1) * GW]
        yoff = _dot(cg, st_old.astype(BF16)) * eg
        bg_t = bm[:, g * N:(g + 1) * N].T.astype(BF16)
        state[g] = st_old * eg[L - 1:L, :] + _dot(bg_t, xw_b[:, g * GW:(g + 1) * GW])
        ys.append(yd + yoff)
    y = jnp.concatenate(ys, axis=1) + xs * dsk_ref[...]
    zf = z_ref[...].astype(F32)
    y = y * (zf * _sigmoid(zf))
    y = jnp.concatenate([_rms(y[:, g * GW:(g + 1) * GW], NORM_EPS) for g in range(G)], axis=1)
    o_ref[...] = (y * nw_ref[...]).astype(o_ref.dtype)


def _ssd(xbc, z, dt, cw, cb, dtb, alog, dsk, nw, emat, batch, seq):
    L = SSD_CHUNK
    nc = seq // L
    row = lambda b, c: (b * nc + c, 0)
    const = lambda b, c: (0, 0)
    return pl.pallas_call(
        _ssd_kernel,
        grid=(batch, nc),
        in_specs=[pl.BlockSpec((L, SSD_CONV_DIM), row), pl.BlockSpec((L, SSD_D_INNER), row),
                  pl.BlockSpec((L, LANES), row),
                  pl.BlockSpec((SSD_CONV, SSD_CONV_DIM), const), pl.BlockSpec((1, SSD_CONV_DIM), const),
                  pl.BlockSpec((1, LANES), const), pl.BlockSpec((1, LANES), const),
                  pl.BlockSpec((1, SSD_D_INNER), const), pl.BlockSpec((1, SSD_D_INNER), const),
                  pl.BlockSpec((LANES, SSD_D_INNER), const)],
        out_specs=pl.BlockSpec((L, SSD_D_INNER), row),
        out_shape=jax.ShapeDtypeStruct((batch * seq, SSD_D_INNER), BF16),
        scratch_shapes=[pltpu.VMEM((SUBLANES + L, SSD_CONV_DIM), F32),
                        pltpu.VMEM((SSD_GROUPS, SSD_STATE, SSD_D_INNER // SSD_GROUPS), F32)],
        compiler_params=_cparams(("arbitrary", "arbitrary")),
        name="ssd",
    )(xbc, z, dt, cw, cb, dtb, alog, dsk, nw, emat)


def _outproj_router_kernel(x_ref, ym_ref, yd_ref, ys_ref, g1_ref, sh_ref, sc_ref, nw_ref, wm_ref, wd_ref, ws_ref,
                           wr_ref, br_ref, xo_ref, h_ref, route_ref, cnt_ref, carry):
    @pl.when(pl.program_id(0) == 0)
    def _():
        carry[...] = jnp.zeros(carry.shape, F32)

    y = _dot(ym_ref[...], wm_ref[...]) + _dot(yd_ref[...], wd_ref[...]) + _dot(ys_ref[...], ws_ref[...])
    xn = x_ref[...] + g1_ref[...] * y
    xo_ref[...] = xn
    h = _rms(xn, NORM_EPS) * nw_ref[...] * (1.0 + sc_ref[...]) + sh_ref[...]
    h_ref[...] = h
    logits = _dot(h, wr_ref[...], precision=HIGHEST) + br_ref[...]
    tm = logits.shape[0]
    lane = lax.broadcasted_iota(jnp.int32, logits.shape, 1)
    big = jnp.int32(1 << 20)
    neg = -jnp.inf
    lg = jnp.where((lane >= N_EXPERTS) & (lane < N_EXPERTS + MOE_GROUPS), logits, neg)
    mg = jnp.max(lg, axis=-1, keepdims=True)
    pg_top = 1.0 / jnp.sum(jnp.exp(lg - mg), axis=-1, keepdims=True)
    gsel = jnp.min(jnp.where(lg == mg, lane, big), axis=-1, keepdims=True) - N_EXPERTS
    le = jnp.where((lane < N_EXPERTS) & ((lane // MOE_EXPERTS_PER_GROUP) == gsel), logits, neg)
    m1 = jnp.max(le, axis=-1, keepdims=True)
    e1 = jnp.min(jnp.where(le == m1, lane, big), axis=-1, keepdims=True)
    le2 = jnp.where(lane == e1, neg, le)
    m2 = jnp.max(le2, axis=-1, keepdims=True)
    e2 = jnp.min(jnp.where(le2 == m2, lane, big), axis=-1, keepdims=True)
    rr = jnp.exp(m2 - m1)
    gate1 = pg_top / (1.0 + rr)
    gate2 = pg_top * rr / (1.0 + rr)
    onehot = jnp.where((lane == e1) | (lane == e2), 1.0, 0.0)
    ri = lax.broadcasted_iota(jnp.int32, (tm, tm), 0)
    ci = lax.broadcasted_iota(jnp.int32, (tm, tm), 1)
    before = jnp.where(ri > ci, 1.0, 0.0).astype(BF16)
    prefix = _dot(before, onehot.astype(BF16)) + carry[...]
    rank1 = jnp.sum(jnp.where(lane == e1, prefix, 0.0), axis=-1, keepdims=True)
    rank2 = jnp.sum(jnp.where(lane == e2, prefix, 0.0), axis=-1, keepdims=True)
    carry[...] = carry[...] + jnp.sum(onehot, axis=0, keepdims=True)
    cnt_ref[...] = carry[...]
    cols = (e1.astype(F32), e2.astype(F32), rank1, rank2, gate1, gate2)
    route = jnp.zeros(logits.shape, F32)
    for i, col in enumerate(cols):
        route = jnp.where(lane == i, col, route)
    route_ref[...] = route


def _outproj_router(xf, ym, yd, ys, g1, sh2, sc2, n2w, wom, wod, wos, wr, br, seq):
    T, D = xf.shape
    tm = 256
    per_b = seq // tm
    row = lambda i: (i, 0)
    const = lambda i: (0, 0)
    bat = lambda i: (i // per_b, 0, 0)
    return pl.pallas_call(
        _outproj_router_kernel,
        grid=(T // tm,),
        in_specs=[pl.BlockSpec((tm, D), row), pl.BlockSpec((tm, 512), row), pl.BlockSpec((tm, 512), row),
                  pl.BlockSpec((tm, SSD_D_INNER), row),
                  pl.BlockSpec((None, 1, D), bat), pl.BlockSpec((None, 1, D), bat), pl.BlockSpec((None, 1, D), bat),
                  pl.BlockSpec((1, D), const),
                  pl.BlockSpec((512, D), const), pl.BlockSpec((512, D), const), pl.BlockSpec((SSD_D_INNER, D), const),
                  pl.BlockSpec((D, LANES), const), pl.BlockSpec((1, LANES), const)],
        out_specs=[pl.BlockSpec((tm, D), row), pl.BlockSpec((tm, D), row), pl.BlockSpec((tm, LANES), row),
                   pl.BlockSpec((1, LANES), const)],
        out_shape=[jax.ShapeDtypeStruct((T, D), F32), jax.ShapeDtypeStruct((T, D), F32),
                   jax.ShapeDtypeStruct((T, LANES), F32), jax.ShapeDtypeStruct((1, LANES), F32)],
        scratch_shapes=[pltpu.VMEM((1, LANES), F32)],
        compiler_params=_cparams(("arbitrary",)),
        name="outproj_router",
    )(xf, ym, yd, ys, g1, sh2, sc2, n2w, wom, wod, wos, wr, br)


def _row_copy(src, dst, i_src, i_dst, sem):
    return pltpu.make_async_copy(src.at[pl.ds(i_src, 1)], dst.at[pl.ds(i_dst, 1)], sem)


def _dispatch_kernel(dest_ref, h_ref, xb_in_ref, xb_ref, sem):
    del xb_in_ref
    tm = h_ref.shape[0]

    def issue(r, _):
        for k in range(2):
            _row_copy(h_ref, xb_ref, r, dest_ref[2 * r + k], sem).start()
        return 0

    lax.fori_loop(0, tm, issue, 0)

    def drain(r, _):
        for k in range(2):
            _row_copy(h_ref, xb_ref, 0, 0, sem).wait()
        return 0

    lax.fori_loop(0, tm, drain, 0)


def _dispatch(dest, h, xb_zero):
    T, D = h.shape
    tm = 256
    return pl.pallas_call(
        _dispatch_kernel,
        grid=(T // tm,),
        in_specs=[pl.BlockSpec((2 * tm,), lambda i: (i,), memory_space=pltpu.SMEM),
                  pl.BlockSpec((tm, D), lambda i: (i, 0)),
                  pl.BlockSpec(memory_space=pl.ANY)],
        out_specs=pl.BlockSpec(memory_space=pl.ANY),
        out_shape=jax.ShapeDtypeStruct(xb_zero.shape, F32),
        scratch_shapes=[pltpu.SemaphoreType.DMA(())],
        input_output_aliases={2: 0},
        compiler_params=_cparams(("arbitrary",)),
        name="moe_dispatch",
    )(dest, h, xb_zero)


def _expert_kernel(bexp_ref, nused_ref, x_ref, wg_ref, wu_ref, wd_ref, o_ref, wg_s, wu_s, wd_s):
    j = pl.program_id(0)
    used = j < nused_ref[0]
    new_expert = (j == 0) | (bexp_ref[j] != bexp_ref[jnp.maximum(j - 1, 0)])

    @pl.when(used & new_expert)
    def _():
        wg_s[...] = wg_ref[...].astype(BF16)
        wu_s[...] = wu_ref[...].astype(BF16)
        wd_s[...] = wd_ref[...].astype(BF16)

    @pl.when(used)
    def _():
        x = x_ref[...].astype(BF16)
        g = _dot(x, wg_s[...])
        u = _dot(x, wu_s[...])
        o_ref[...] = _dot((g * _sigmoid(g) * u).astype(BF16), wd_s[...])

    @pl.when(jnp.logical_not(used))
    def _():
        o_ref[...] = jnp.zeros(o_ref.shape, F32)


def _experts(bexp, nused, xb, w_gate, w_up, w_down, layer):
    cap, D = xb.shape
    nb = cap // MOE_ROWS
    F = D_EXPERT
    last = lambda j, be, nu: jnp.minimum(j, nu[0] - 1)
    return pl.pallas_call(
        _expert_kernel,
        grid_spec=pltpu.PrefetchScalarGridSpec(
            num_scalar_prefetch=2,
            grid=(nb,),
            in_specs=[pl.BlockSpec((MOE_ROWS, D), lambda j, be, nu: (last(j, be, nu), 0)),
                      pl.BlockSpec((None, None, D, F), lambda j, be, nu: (layer, be[last(j, be, nu)], 0, 0)),
                      pl.BlockSpec((None, None, D, F), lambda j, be, nu: (layer, be[last(j, be, nu)], 0, 0)),
                      pl.BlockSpec((None, None, F, D), lambda j, be, nu: (layer, be[last(j, be, nu)], 0, 0))],
            out_specs=pl.BlockSpec((MOE_ROWS, D), lambda j, be, nu: (j, 0)),
            scratch_shapes=[pltpu.VMEM((D, F), BF16), pltpu.VMEM((D, F), BF16), pltpu.VMEM((F, D), BF16)],
        ),
        out_shape=jax.ShapeDtypeStruct((cap, D), F32),
        compiler_params=_cparams(("arbitrary",)),
        name="moe_experts",
    )(bexp, nused, xb, w_gate, w_up, w_down)


def _combine_kernel(dest_ref, x_ref, route_ref, g2_ref, fnw_ref, yb_ref, o_ref, gbuf, sem, *, final):
    tm = x_ref.shape[0]

    def issue(r, _):
        for k in range(2):
            _row_copy(yb_ref, gbuf.at[k], dest_ref[2 * r + k], r, sem).start()
        return 0

    lax.fori_loop(0, tm, issue, 0)

    def drain(r, _):
        for k in range(2):
            _row_copy(yb_ref, gbuf.at[k], 0, 0, sem).wait()
        return 0

    lax.fori_loop(0, tm, drain, 0)
    route = route_ref[...]
    y = route[:, 4:5] * gbuf[0] + route[:, 5:6] * gbuf[1]
    xo = x_ref[...] + g2_ref[...] * y
    if final:
        xo = _rms(xo, NORM_EPS) * fnw_ref[...]
    o_ref[...] = xo


def _combine(dest, xf, route, g2, fnw, yb, seq, final):
    T, D = xf.shape
    tm = 256
    per_b = seq // tm
    return pl.pallas_call(
        functools.partial(_combine_kernel, final=final),
        grid=(T // tm,),
        in_specs=[pl.BlockSpec((2 * tm,), lambda i: (i,), memory_space=pltpu.SMEM),
                  pl.BlockSpec((tm, D), lambda i: (i, 0)),
                  pl.BlockSpec((tm, LANES), lambda i: (i, 0)),
                  pl.BlockSpec((None, 1, D), lambda i: (i // per_b, 0, 0)),
                  pl.BlockSpec((1, D), lambda i: (0, 0)),
                  pl.BlockSpec(memory_space=pl.ANY)],
        out_specs=pl.BlockSpec((tm, D), lambda i: (i, 0)),
        out_shape=jax.ShapeDtypeStruct((T, D), F32),
        scratch_shapes=[pltpu.VMEM((2, tm, D), F32), pltpu.SemaphoreType.DMA(())],
        compiler_params=_cparams(("arbitrary",)),
        name="moe_combine",
    )(dest, xf, route, g2, fnw, yb)


def _rot_cols(w):
    half = w.shape[1] // 2
    return jnp.concatenate([-w[:, half:], w[:, :half]], axis=1)


def _pad_lanes(a, n=LANES):
    return jnp.pad(a, [(0, 0)] * (a.ndim - 1) + [(0, n - a.shape[-1])])


def kernel(x, c, positions, w_ada, b_ada, norm1_w, w_in, mla_q_norm_w, mla_w_uq, mla_kv_norm_w, mla_w_ukv, diff_lambda, diff_subln_w, ssd_conv_w, ssd_conv_b, ssd_dt_bias, ssd_a_log, ssd_d, ssd_norm_w, w_out, norm2_w, router_w_group, router_b_group, router_w_expert, router_b_expert, exp_w_gate, exp_w_up, exp_w_down, final_norm_w):
    B, S, D = x.shape
    T = B * S
    xf = x.reshape(T, D)
    mod = _ada(jnp.pad(c, ((0, SUBLANES - B), (0, 0))), w_ada, b_ada)[:, :B]

    pos_f = positions.astype(F32)
    pos_col = pos_f.reshape(T, 1)
    inv = 1.0 / (ROPE_THETA ** (jnp.arange(0, MLA_D_ROPE, 2, dtype=F32) / MLA_D_ROPE))
    inv = jnp.tile(inv, 4).reshape(1, LANES)
    slopes = jnp.exp2(-8.0 / DIFF_HEADS * jnp.arange(1, DIFF_HEADS + 1, dtype=F32)).reshape(DIFF_HEADS, 1, 1)
    slopes = slopes * LOG2E
    tq = min(FLASH_TQ, S)
    head_of_col = np.arange(SSD_D_INNER) // SSD_HEAD_DIM
    emat = jnp.asarray(np.arange(LANES)[:, None] == head_of_col[None, :], BF16)

    cap = T * 2 + N_EXPERTS * MOE_ROWS
    nb = cap // MOE_ROWS
    xb_zero = jnp.zeros((cap, D), F32)
    fnw = final_norm_w.reshape(1, D)

    for l in range(DEPTH):
        sh1, sc1, g1, sh2, sc2, g2 = [mod[l, :, i * D:(i + 1) * D].reshape(B, 1, D) for i in range(6)]
        w = w_in[l]
        ws = [jnp.concatenate([w[:, :_OFF[3]], _rot_cols(w[:, _OFF[2]:_OFF[3]])], axis=1)]
        ws.append(w[:, _OFF[3]:_OFF[4]] * (DIFF_D_QK ** -0.5 * LOG2E))
        ws += [w[:, _OFF[i]:_OFF[i + 1]] for i in range(4, 8)]
        ws.append(_pad_lanes(w[:, _OFF[8]:_OFF[9]]))
        ws = [a.astype(BF16) for a in ws]
        a_mla, dq, dk, dv, z, xbc, dt = _inproj(xf, norm1_w[l].reshape(1, D), sh1, sc1, ws, S)

        wq = mla_w_uq[l].reshape(MLA_Q_LORA, MLA_HEADS, MLA_D_NOPE + MLA_D_ROPE)
        wq_pe = wq[:, :, MLA_D_NOPE:]
        wq_rot = jnp.concatenate([-wq_pe[:, :, MLA_D_ROPE // 2:], wq_pe[:, :, :MLA_D_ROPE // 2]], axis=2)
        wq_cat = jnp.concatenate([wq, wq_rot], axis=2).reshape(MLA_Q_LORA, MLA_HEADS * 256).astype(BF16)
        q, k, v = _mla_prep(a_mla, pos_col, inv, mla_q_norm_w[l].reshape(1, -1), mla_kv_norm_w[l].reshape(1, -1),
                            wq_cat, mla_w_ukv[l].astype(BF16))
        q_t = q.transpose(0, 2, 1)
        v_t = v.reshape(MLA_HEADS, T // tq, tq, MLA_D_V).transpose(0, 1, 3, 2)
        y_mla = _mla_flash(q_t, k, v_t, B, S, tq)

        lam_init = 0.8 - 0.6 * math.exp(-0.3 * l)
        dv_t = dv.reshape(T // tq, tq, DIFF_HEADS * DIFF_D_V).transpose(0, 2, 1)
        y_diff = _diff_flash(dq.T, dk, dv_t, pos_f.reshape(B, 1, S), pos_f.reshape(T // tq, tq, 1), slopes,
                             diff_lambda[l], diff_subln_w[l].reshape(-1, 1), B, S, tq, lam_init)

        y_ssd = _ssd(xbc, z, dt, ssd_conv_w[l], ssd_conv_b[l].reshape(1, -1),
                     _pad_lanes(ssd_dt_bias[l].reshape(1, -1)), _pad_lanes(ssd_a_log[l].reshape(1, -1)),
                     jnp.repeat(ssd_d[l], SSD_HEAD_DIM).reshape(1, -1), ssd_norm_w[l].reshape(1, -1), emat, B, S)

        wo = w_out[l].astype(BF16)
        wr = _pad_lanes(jnp.concatenate([router_w_expert[l], router_w_group[l]], axis=1))
        br = _pad_lanes(jnp.concatenate([router_b_expert[l], router_b_group[l]]).reshape(1, -1))
        x_mid, h2, route, cnt = _outproj_router(xf, y_mla, y_diff, y_ssd, g1, sh2, sc2, norm2_w[l].reshape(1, D),
                                                wo[:512], wo[512:1024], wo[1024:], wr, br, S)

        counts = cnt[0, :N_EXPERTS].astype(jnp.int32)
        pcounts = (counts + MOE_ROWS - 1) // MOE_ROWS * MOE_ROWS
        pend = jnp.cumsum(pcounts)
        pstart = pend - pcounts
        eid = route[:, 0:2].astype(jnp.int32)
        dest = (pstart[eid] + route[:, 2:4].astype(jnp.int32)).reshape(-1)
        starts = jnp.arange(nb, dtype=jnp.int32) * MOE_ROWS
        bexp = jnp.minimum(jnp.sum(pend[None, :] <= starts[:, None], axis=1), N_EXPERTS - 1).astype(jnp.int32)
        nused = (pend[-1:] // MOE_ROWS).astype(jnp.int32)

        xb = _dispatch(dest, h2, xb_zero)
        yb = _experts(bexp, nused, xb, exp_w_gate, exp_w_up, exp_w_down, l)
        xf = _combine(dest, x_mid, route, g2, fnw, yb, S, final=(l == DEPTH - 1))
    return xf.reshape(B, S, D)
```

```python
import functools
import math

import numpy as np
import jax
import jax.numpy as jnp
from jax import lax
from jax.experimental import pallas as pl
from jax.experimental.pallas import tpu as pltpu

F32 = jnp.float32
BF16 = jnp.bfloat16
HIGHEST = lax.Precision.HIGHEST

D_MODEL = 1024
DEPTH = 2
MLA_HEADS = 4
MLA_Q_LORA = 256
MLA_KV_LORA = 128
MLA_D_NOPE = 128
MLA_D_ROPE = 64
MLA_D_V = 128
ROPE_THETA = 10000.0
DIFF_HEADS = 4
DIFF_D_QK = 64
DIFF_D_V = 128
SSD_HEADS = 16
SSD_HEAD_DIM = 64
SSD_D_INNER = 1024
SSD_GROUPS = 4
SSD_STATE = 128
SSD_CONV = 4
SSD_CHUNK = 128
SSD_CONV_DIM = SSD_D_INNER + 2 * SSD_GROUPS * SSD_STATE
MOE_GROUPS = 4
MOE_EXPERTS_PER_GROUP = 8
N_EXPERTS = 32
D_EXPERT = 512
NORM_EPS = 1e-6
SUBLN_EPS = 1e-5

LANES = 128
SUBLANES = 8
MOE_ROWS = 256
FLASH_TQ = 512
LOG2E = math.log2(math.e)
VMEM_LIMIT = 48 * 1024 * 1024

_OFF = np.cumsum([0, MLA_Q_LORA, MLA_KV_LORA, MLA_D_ROPE, 512, 512, 512, SSD_D_INNER, SSD_CONV_DIM, SSD_HEADS])


def _cparams(sem):
    return pltpu.CompilerParams(dimension_semantics=sem, vmem_limit_bytes=VMEM_LIMIT)


def _dot(a, b, **kw):
    return jnp.dot(a, b, preferred_element_type=F32, **kw)


def _dot_nt(a, b):
    return lax.dot_general(a, b, (((1,), (1,)), ((), ())), preferred_element_type=F32)


def _rms(x, eps):
    return x * lax.rsqrt(jnp.mean(x * x, axis=-1, keepdims=True) + eps)


def _sigmoid(x):
    return 1.0 / (1.0 + jnp.exp(-x))


def _ada_kernel(c_ref, w_ref, b_ref, o_ref):
    c = c_ref[...]
    cond = c * _sigmoid(c)
    o_ref[...] = _dot(cond, w_ref[...], precision=HIGHEST) + b_ref[...]


def _ada(c_pad, w_ada, b_ada):
    L, D, N = w_ada.shape
    tn = 1024
    return pl.pallas_call(
        _ada_kernel,
        grid=(L, N // tn),
        in_specs=[
            pl.BlockSpec((SUBLANES, D), lambda l, j: (0, 0)),
            pl.BlockSpec((None, D, tn), lambda l, j: (l, 0, j)),
            pl.BlockSpec((None, 1, tn), lambda l, j: (l, 0, j)),
        ],
        out_specs=pl.BlockSpec((None, SUBLANES, tn), lambda l, j: (l, 0, j)),
        out_shape=jax.ShapeDtypeStruct((L, SUBLANES, N), F32),
        compiler_params=_cparams(("parallel", "parallel")),
        name="ada",
    )(c_pad, w_ada, b_ada.reshape(L, 1, N))


def _inproj_kernel(x_ref, nw_ref, sh_ref, sc_ref, wm_ref, wq_ref, wk_ref, wv_ref, wz_ref, wx_ref, wt_ref,
                   om_ref, oq_ref, ok_ref, ov_ref, oz_ref, ox_ref, ot_ref):
    x = x_ref[...]
    h = _rms(x, NORM_EPS) * nw_ref[...] * (1.0 + sc_ref[...]) + sh_ref[...]
    hb = h.astype(BF16)
    for w_ref, o_ref in ((wm_ref, om_ref), (wq_ref, oq_ref), (wk_ref, ok_ref), (wv_ref, ov_ref),
                         (wz_ref, oz_ref), (wx_ref, ox_ref), (wt_ref, ot_ref)):
        o_ref[...] = _dot(hb, w_ref[...]).astype(o_ref.dtype)


def _inproj(xf, nw, sh, sc, ws, seq):
    T, D = xf.shape
    tm = 256
    per_b = seq // tm
    widths = [w.shape[1] for w in ws]
    row = lambda i: (i, 0)
    const = lambda i: (0, 0)
    bat = lambda i: (i // per_b, 0, 0)
    in_specs = [pl.BlockSpec((tm, D), row), pl.BlockSpec((1, D), const),
                pl.BlockSpec((None, 1, D), bat), pl.BlockSpec((None, 1, D), bat)]
    in_specs += [pl.BlockSpec((D, n), const) for n in widths]
    out_dt = [BF16] * 6 + [F32]
    return pl.pallas_call(
        _inproj_kernel,
        grid=(T // tm,),
        in_specs=in_specs,
        out_specs=[pl.BlockSpec((tm, n), row) for n in widths],
        out_shape=[jax.ShapeDtypeStruct((T, n), dt) for n, dt in zip(widths, out_dt)],
        compiler_params=_cparams(("parallel",)),
        name="inproj",
    )(xf, nw, sh, sc, *ws)


def _mla_prep_kernel(a_ref, pos_ref, inv_ref, qnw_ref, kvnw_ref, wq_ref, wkv_ref, q_ref, k_ref, v_ref):
    a = a_ref[...].astype(F32)
    cqn = (_rms(a[:, :256], NORM_EPS) * qnw_ref[...]).astype(BF16)
    ckvn = (_rms(a[:, 256:384], NORM_EPS) * kvnw_ref[...]).astype(BF16)
    ang = pos_ref[...] * inv_ref[...]
    lane = lax.broadcasted_iota(jnp.int32, ang.shape, 1)
    cs = jnp.where(lane < MLA_D_ROPE, jnp.cos(ang), jnp.sin(ang))
    kp = a[:, 384:512] * cs
    krkr = kp + pltpu.roll(kp, MLA_D_ROPE, axis=1)
    scale = (MLA_D_NOPE + MLA_D_ROPE) ** -0.5 * LOG2E
    qall = _dot(cqn, wq_ref[...]) * scale
    kvall = _dot(ckvn, wkv_ref[...])
    for h in range(MLA_HEADS):
        o = h * 256
        q_ref[h] = jnp.concatenate([qall[:, o:o + 128], qall[:, o + 128:o + 256] * cs], axis=1).astype(BF16)
        k_ref[h] = jnp.concatenate([kvall[:, o:o + 128], krkr], axis=1).astype(BF16)
        v_ref[h] = kvall[:, o + 128:o + 256].astype(BF16)


def _mla_prep(a, pos_col, inv, qnw, kvnw, wq, wkv):
    T = a.shape[0]
    tm = 512
    row = lambda i: (i, 0)
    const = lambda i: (0, 0)
    hrow = lambda i: (0, i, 0)
    return pl.pallas_call(
        _mla_prep_kernel,
        grid=(T // tm,),
        in_specs=[pl.BlockSpec((tm, 512), row), pl.BlockSpec((tm, 1), row), pl.BlockSpec((1, LANES), const),
                  pl.BlockSpec((1, 256), const), pl.BlockSpec((1, 128), const),
                  pl.BlockSpec((256, 1024), const), pl.BlockSpec((128, 1024), const)],
        out_specs=[pl.BlockSpec((MLA_HEADS, tm, 256), hrow), pl.BlockSpec((MLA_HEADS, tm, 256), hrow),
                   pl.BlockSpec((MLA_HEADS, tm, 128), hrow)],
        out_shape=[jax.ShapeDtypeStruct((MLA_HEADS, T, 256), BF16), jax.ShapeDtypeStruct((MLA_HEADS, T, 256), BF16),
                   jax.ShapeDtypeStruct((MLA_HEADS, T, 128), BF16)],
        compiler_params=_cparams(("parallel",)),
        name="mla_prep",
    )(a, pos_col, inv, qnw, kvnw, wq, wkv)


FLASH_STRIP = LANES


def _flash_init(m_s, l_s, acc_s, p_s, a_s):
    m_s[...] = jnp.full(m_s.shape, -jnp.inf, F32)
    l_s[...] = jnp.zeros(l_s.shape, F32)
    acc_s[...] = jnp.zeros(acc_s.shape, F32)
    p_s[1] = jnp.zeros(p_s.shape[1:], BF16)
    a_s[1] = jnp.ones(a_s.shape[1:], F32)


def _flash_pv(v_t, slot, acc_s, p_s, a_s):
    acc_s[...] = a_s[slot] * acc_s[...] + _dot(v_t, p_s[slot])


def _flash_block(s_t, strip_fn, v_prev, src, dst, m_s, l_s, acc_s, p_s, a_s):
    _flash_pv(v_prev, src, acc_s, p_s, a_s)
    for j in range(m_s.shape[1] // FLASH_STRIP):
        sl = slice(j * FLASH_STRIP, (j + 1) * FLASH_STRIP)
        t = strip_fn(s_t, j)
        m_old = m_s[:, sl]
        m_new = jnp.maximum(m_old, jnp.max(t, axis=0, keepdims=True))
        alpha = jnp.exp2(m_old - m_new)
        p = jnp.exp2(t - m_new)
        l_s[:, sl] = alpha * l_s[:, sl] + jnp.sum(p, axis=0, keepdims=True)
        m_s[:, sl] = m_new
        p_s[dst, :, sl] = p.astype(BF16)
        a_s[dst, :, sl] = alpha


def _flash_run(qi, block, flush):
    def pair(i, _):
        block(2 * i, False, 1, 0)
        block(2 * i + 1, False, 0, 1)
        return 0

    lax.fori_loop(0, qi // 2, pair, 0)

    @pl.when(qi % 2 == 0)
    def _():
        block(qi, True, 1, 0)
        flush(qi, 0)

    @pl.when(qi % 2 == 1)
    def _():
        block(qi - 1, False, 1, 0)
        block(qi, True, 0, 1)
        flush(qi, 1)


def _causal_strip(t, j):
    r = lax.broadcasted_iota(jnp.int32, t.shape, 0)
    c = lax.broadcasted_iota(jnp.int32, t.shape, 1) + j * FLASH_STRIP
    return jnp.where(r <= c, t, -jnp.inf)


def _flash_scratch(dv, tk, nq):
    return [pltpu.VMEM((1, nq), F32), pltpu.VMEM((1, nq), F32), pltpu.VMEM((dv, nq), F32),
            pltpu.VMEM((2, tk, nq), BF16), pltpu.VMEM((2, 1, nq), F32)]


def _mla_flash_kernel(qt_ref, k_ref, vt_ref, o_ref, m_s, l_s, acc_s, p_s, a_s, *, tq):
    qi = pl.program_id(2)
    state = (m_s, l_s, acc_s, p_s, a_s)
    _flash_init(*state)
    q_t = qt_ref[...]

    def block(kb, masked, src, dst):
        start = pl.multiple_of(kb * tq, tq)
        s_t = _dot(k_ref[pl.ds(start, tq), :], q_t)

        def strip(s, j):
            t = s[:, j * FLASH_STRIP:(j + 1) * FLASH_STRIP]
            return _causal_strip(t, j) if masked else t

        _flash_block(s_t, strip, vt_ref[jnp.maximum(kb - 1, 0)], src, dst, *state)

    _flash_run(qi, block, lambda kb, slot: _flash_pv(vt_ref[kb], slot, acc_s, p_s, a_s))
    o_ref[...] = (acc_s[...] * (1.0 / l_s[...])).T.astype(o_ref.dtype)


def _mla_flash(q_t, k, v_t, batch, seq, tq):
    nq = seq // tq
    return pl.pallas_call(
        functools.partial(_mla_flash_kernel, tq=tq),
        grid=(batch, MLA_HEADS, nq),
        in_specs=[pl.BlockSpec((None, 256, tq), lambda b, h, i: (h, 0, b * nq + i)),
                  pl.BlockSpec((None, seq, 256), lambda b, h, i: (h, b, 0)),
                  pl.BlockSpec((None, nq, MLA_D_V, tq), lambda b, h, i: (h, b, 0, 0))],
        out_specs=pl.BlockSpec((tq, MLA_D_V), lambda b, h, i: (b * nq + i, h)),
        out_shape=jax.ShapeDtypeStruct((batch * seq, MLA_HEADS * MLA_D_V), BF16),
        scratch_shapes=_flash_scratch(MLA_D_V, tq, tq),
        compiler_params=_cparams(("parallel", "parallel", "arbitrary")),
        name="mla_flash",
    )(q_t, k, v_t)


def _diff_flash_kernel(qt_ref, k_ref, vt_ref, pq_ref, pk_ref, slope_ref, lam_ref, sw_ref, o_ref,
                       m_s, l_s, acc_s, p_s, a_s, *, tq, lam_init):
    qi = pl.program_id(2)
    state = (m_s, l_s, acc_s, p_s, a_s)
    _flash_init(*state)
    q_t = qt_ref[...]
    row = lax.broadcasted_iota(jnp.int32, q_t.shape, 0)
    zero = jnp.zeros_like(q_t)
    q2_t = jnp.concatenate([jnp.where(row < DIFF_D_QK, q_t, zero), jnp.where(row >= DIFF_D_QK, q_t, zero)], axis=1)
    pq = pq_ref[...]
    slope = slope_ref[...]
    spm = tq // FLASH_STRIP

    def block(kb, masked, src, dst):
        start = pl.multiple_of(kb * tq, tq)
        s_t = _dot(k_ref[pl.ds(start, tq), :], q2_t)
        bias = slope * jnp.abs(pk_ref[kb] - pq)

        def strip(s, j):
            jq = j % spm
            t = s[:, j * FLASH_STRIP:(j + 1) * FLASH_STRIP] - bias[:, jq * FLASH_STRIP:(jq + 1) * FLASH_STRIP]
            return _causal_strip(t, jq) if masked else t

        _flash_block(s_t, strip, vt_ref[jnp.maximum(kb - 1, 0)], src, dst, *state)

    _flash_run(qi, block, lambda kb, slot: _flash_pv(vt_ref[kb], slot, acc_s, p_s, a_s))
    o2 = acc_s[...] * (1.0 / l_s[...])
    lf = lam_ref[...]
    lam = (jnp.exp(jnp.sum(lf[0:1] * lf[1:2], keepdims=True))
           - jnp.exp(jnp.sum(lf[2:3] * lf[3:4], keepdims=True)) + lam_init)
    out = o2[:, :tq] - lam * o2[:, tq:]
    out = out * lax.rsqrt(jnp.mean(out * out, axis=0, keepdims=True) + SUBLN_EPS) * sw_ref[...] * (1.0 - lam_init)
    o_ref[...] = out.T.astype(o_ref.dtype)


def _diff_flash(q_t, k, v_t, pos_row, pos_kcol, slopes, lam, sw_col, batch, seq, tq, lam_init):
    nq = seq // tq
    return pl.pallas_call(
        functools.partial(_diff_flash_kernel, tq=tq, lam_init=lam_init),
        grid=(batch, DIFF_HEADS, nq),
        in_specs=[pl.BlockSpec((128, tq), lambda b, h, i: (h, b * nq + i)),
                  pl.BlockSpec((seq, 128), lambda b, h, i: (b, h)),
                  pl.BlockSpec((nq, DIFF_D_V, tq), lambda b, h, i: (b, h, 0)),
                  pl.BlockSpec((None, 1, tq), lambda b, h, i: (b, 0, i)),
                  pl.BlockSpec((nq, tq, 1), lambda b, h, i: (b, 0, 0)),
                  pl.BlockSpec((None, 1, 1), lambda b, h, i: (h, 0, 0)),
                  pl.BlockSpec((4, DIFF_D_QK), lambda b, h, i: (0, 0)),
                  pl.BlockSpec((DIFF_D_V, 1), lambda b, h, i: (0, 0))],
        out_specs=pl.BlockSpec((tq, DIFF_D_V), lambda b, h, i: (b * nq + i, h)),
        out_shape=jax.ShapeDtypeStruct((batch * seq, DIFF_HEADS * DIFF_D_V), BF16),
        scratch_shapes=_flash_scratch(DIFF_D_V, tq, 2 * tq),
        compiler_params=_cparams(("parallel", "parallel", "arbitrary")),
        name="diff_flash",
    )(q_t, k, v_t, pos_row, pos_kcol, slopes, lam, sw_col)


def _ssd_kernel(xbc_ref, z_ref, dt_ref, cw_ref, cb_ref, dtb_ref, alog_ref, dsk_ref, nw_ref, e_ref, o_ref,
                ext, state):
    L = SSD_CHUNK
    G, N = SSD_GROUPS, SSD_STATE
    GW = SSD_D_INNER // G

    @pl.when(pl.program_id(1) == 0)
    def _():
        ext[0:SUBLANES, :] = jnp.zeros((SUBLANES, SSD_CONV_DIM), F32)
        state[...] = jnp.zeros(state.shape, F32)

    u = xbc_ref[...].astype(F32)
    ext[SUBLANES:SUBLANES + L, :] = u
    acc = cb_ref[...] + cw_ref[SSD_CONV - 1:SSD_CONV, :] * u
    for k in range(SSD_CONV - 1):
        o = SUBLANES - (SSD_CONV - 1) + k
        acc = acc + cw_ref[k:k + 1, :] * ext[o:o + L, :]
    ext[0:SUBLANES, :] = u[L - SUBLANES:L, :]
    act = acc * _sigmoid(acc)
    xs = act[:, :SSD_D_INNER]
    bm = act[:, SSD_D_INNER:SSD_D_INNER + G * N]
    cm = act[:, SSD_D_INNER + G * N:]

    dtr = dt_ref[...] + dtb_ref[...]
    dtv = jnp.maximum(dtr, 0.0) + jnp.log1p(jnp.exp(-jnp.abs(dtr)))
    adt = dtv * (-jnp.exp(alog_ref[...]))
    r = lax.broadcasted_iota(jnp.int32, (L, L), 0)
    c = lax.broadcasted_iota(jnp.int32, (L, L), 1)
    tril = r >= c
    a_cs = _dot(tril.astype(F32), adt, precision=HIGHEST)
    a_cs_t = a_cs.T
    ea = jnp.exp(a_cs)
    dte = jnp.exp(a_cs[L - 1:L, :] - a_cs)
    stack = jnp.concatenate([dtv, dte, ea], axis=0)
    hi = stack.astype(BF16)
    lo = (stack - hi.astype(F32)).astype(BF16)
    ex = _dot(hi, e_ref[...]) + _dot(lo, e_ref[...])
    dt_e, dte_e, ea_e = ex[:L], ex[L:2 * L], ex[2 * L:]
    xdt = xs * dt_e
    xdt_b = xdt.astype(BF16)
    xw_b = (xdt * dte_e).astype(BF16)
    bb = bm.astype(BF16)
    cbf = cm.astype(BF16)
    lane_g = lax.broadcasted_iota(jnp.int32, (1, GW), 1)
    ys = []
    for g in range(G):
        bg = bb[:, g * N:(g + 1) * N]
        cg = cbf[:, g * N:(g + 1) * N]
        cb = _dot_nt(cg, bg)
        xg = xdt_b[:, g * GW:(g + 1) * GW]
        yd = jnp.zeros((L, GW), F32)
        for rr in range(SSD_HEADS // G):
            h = g * (SSD_HEADS // G) + rr
            seg = a_cs[:, h:h + 1] - a_cs_t[h:h + 1, :]
            dec = jnp.exp(jnp.where(tril, seg, -jnp.inf))
            in_head = (lane_g >= rr * SSD_HEAD_DIM) & (lane_g < (rr + 1) * SSD_HEAD_DIM)
            yd = yd + _dot((cb * dec).astype(BF16), jnp.where(in_head, xg, jnp.zeros_like(xg)))
        st_old = state[g]
        eg = ea_e[:, g * GW:(g + 1) * GW]
        yoff = _dot(cg, st_old.astype(BF16)) * eg
        bg_t = bm[:, g * N:(g + 1) * N].T.astype(BF16)
        state[g] = st_old * eg[L - 1:L, :] + _dot(bg_t, xw_b[:, g * GW:(g + 1) * GW])
        ys.append(yd + yoff)
    y = jnp.concatenate(ys, axis=1) + xs * dsk_ref[...]
    zf = z_ref[...].astype(F32)
    y = y * (zf * _sigmoid(zf))
    y = jnp.concatenate([_rms(y[:, g * GW:(g + 1) * GW], NORM_EPS) for g in range(G)], axis=1)
    o_ref[...] = (y * nw_ref[...]).astype(o_ref.dtype)


def _ssd(xbc, z, dt, cw, cb, dtb, alog, dsk, nw, emat, batch, seq):
    L = SSD_CHUNK
    nc = seq // L
    row = lambda b, c: (b * nc + c, 0)
    const = lambda b, c: (0, 0)
    return pl.pallas_call(
        _ssd_kernel,
        grid=(batch, nc),
        in_specs=[pl.BlockSpec((L, SSD_CONV_DIM), row), pl.BlockSpec((L, SSD_D_INNER), row),
                  pl.BlockSpec((L, LANES), row),
                  pl.BlockSpec((SSD_CONV, SSD_CONV_DIM), const), pl.BlockSpec((1, SSD_CONV_DIM), const),
                  pl.BlockSpec((1, LANES), const), pl.BlockSpec((1, LANES), const),
                  pl.BlockSpec((1, SSD_D_INNER), const), pl.BlockSpec((1, SSD_D_INNER), const),
                  pl.BlockSpec((LANES, SSD_D_INNER), const)],
        out_specs=pl.BlockSpec((L, SSD_D_INNER), row),
        out_shape=jax.ShapeDtypeStruct((batch * seq, SSD_D_INNER), BF16),
        scratch_shapes=[pltpu.VMEM((SUBLANES + L, SSD_CONV_DIM), F32),
                        pltpu.VMEM((SSD_GROUPS, SSD_STATE, SSD_D_INNER // SSD_GROUPS), F32)],
        compiler_params=_cparams(("arbitrary", "arbitrary")),
        name="ssd",
    )(xbc, z, dt, cw, cb, dtb, alog, dsk, nw, emat)


def _outproj_router_kernel(x_ref, ym_ref, yd_ref, ys_ref, g1_ref, sh_ref, sc_ref, nw_ref, wm_ref, wd_ref, ws_ref,
                           wr_ref, br_ref, xo_ref, h_ref, route_ref, cnt_ref, carry):
    @pl.when(pl.program_id(0) == 0)
    def _():
        carry[...] = jnp.zeros(carry.shape, F32)

    y = _dot(ym_ref[...], wm_ref[...]) + _dot(yd_ref[...], wd_ref[...]) + _dot(ys_ref[...], ws_ref[...])
    xn = x_ref[...] + g1_ref[...] * y
    xo_ref[...] = xn
    h = _rms(xn, NORM_EPS) * nw_ref[...] * (1.0 + sc_ref[...]) + sh_ref[...]
    h_ref[...] = h
    h_hi = h.astype(BF16)
    h_lo = (h - h_hi.astype(F32)).astype(BF16)
    logits = _dot(h_hi, wr_ref[0]) + (_dot(h_lo, wr_ref[0]) + _dot(h_hi, wr_ref[1])) + br_ref[...]
    tm = logits.shape[0]
    lane = lax.broadcasted_iota(jnp.int32, logits.shape, 1)
    big = jnp.int32(1 << 20)
    neg = -jnp.inf
    lg = jnp.where((lane >= N_EXPERTS) & (lane < N_EXPERTS + MOE_GROUPS), logits, neg)
    mg = jnp.max(lg, axis=-1, keepdims=True)
    pg_top = 1.0 / jnp.sum(jnp.exp(lg - mg), axis=-1, keepdims=True)
    gsel = jnp.min(jnp.where(lg == mg, lane, big), axis=-1, keepdims=True) - N_EXPERTS
    le = jnp.where((lane < N_EXPERTS) & ((lane // MOE_EXPERTS_PER_GROUP) == gsel), logits, neg)
    m1 = jnp.max(le, axis=-1, keepdims=True)
    e1 = jnp.min(jnp.where(le == m1, lane, big), axis=-1, keepdims=True)
    le2 = jnp.where(lane == e1, neg, le)
    m2 = jnp.max(le2, axis=-1, keepdims=True)
    e2 = jnp.min(jnp.where(le2 == m2, lane, big), axis=-1, keepdims=True)
    rr = jnp.exp(m2 - m1)
    gate1 = pg_top / (1.0 + rr)
    gate2 = pg_top * rr / (1.0 + rr)
    onehot = jnp.where((lane == e1) | (lane == e2), 1.0, 0.0)
    ri = lax.broadcasted_iota(jnp.int32, (tm, tm), 0)
    ci = lax.broadcasted_iota(jnp.int32, (tm, tm), 1)
    before = jnp.where(ri > ci, 1.0, 0.0).astype(BF16)
    prefix = _dot(before, onehot.astype(BF16)) + carry[...]
    rank1 = jnp.sum(jnp.where(lane == e1, prefix, 0.0), axis=-1, keepdims=True)
    rank2 = jnp.sum(jnp.where(lane == e2, prefix, 0.0), axis=-1, keepdims=True)
    carry[...] = carry[...] + jnp.sum(onehot, axis=0, keepdims=True)
    cnt_ref[...] = carry[...]
    cols = (e1.astype(F32), e2.astype(F32), rank1, rank2, gate1, gate2)
    route = jnp.zeros(logits.shape, F32)
    for i, col in enumerate(cols):
        route = jnp.where(lane == i, col, route)
    route_ref[...] = route


def _outproj_router(xf, ym, yd, ys, g1, sh2, sc2, n2w, wom, wod, wos, wr, br, seq):
    T, D = xf.shape
    tm = 256
    per_b = seq // tm
    row = lambda i: (i, 0)
    const = lambda i: (0, 0)
    bat = lambda i: (i // per_b, 0, 0)
    return pl.pallas_call(
        _outproj_router_kernel,
        grid=(T // tm,),
        in_specs=[pl.BlockSpec((tm, D), row), pl.BlockSpec((tm, 512), row), pl.BlockSpec((tm, 512), row),
                  pl.BlockSpec((tm, SSD_D_INNER), row),
                  pl.BlockSpec((None, 1, D), bat), pl.BlockSpec((None, 1, D), bat), pl.BlockSpec((None, 1, D), bat),
                  pl.BlockSpec((1, D), const),
                  pl.BlockSpec((512, D), const), pl.BlockSpec((512, D), const), pl.BlockSpec((SSD_D_INNER, D), const),
                  pl.BlockSpec((2, D, LANES), lambda i: (0, 0, 0)), pl.BlockSpec((1, LANES), const)],
        out_specs=[pl.BlockSpec((tm, D), row), pl.BlockSpec((tm, D), row), pl.BlockSpec((tm, LANES), row),
                   pl.BlockSpec((1, LANES), const)],
        out_shape=[jax.ShapeDtypeStruct((T, D), F32), jax.ShapeDtypeStruct((T, D), F32),
                   jax.ShapeDtypeStruct((T, LANES), F32), jax.ShapeDtypeStruct((1, LANES), F32)],
        scratch_shapes=[pltpu.VMEM((1, LANES), F32)],
        compiler_params=_cparams(("arbitrary",)),
        name="outproj_router",
    )(xf, ym, yd, ys, g1, sh2, sc2, n2w, wom, wod, wos, wr, br)


def _row_copy(src, dst, i_src, i_dst, sem):
    return pltpu.make_async_copy(src.at[pl.ds(i_src, 1)], dst.at[pl.ds(i_dst, 1)], sem)


def _dispatch_kernel(dest_ref, h_ref, xb_in_ref, xb_ref, sem):
    del xb_in_ref
    tm = h_ref.shape[0]

    def issue(r, _):
        for k in range(2):
            _row_copy(h_ref, xb_ref, r, dest_ref[2 * r + k], sem).start()
        return 0

    lax.fori_loop(0, tm, issue, 0, unroll=8)

    def drain(r, _):
        for k in range(2):
            _row_copy(h_ref, xb_ref, 0, 0, sem).wait()
        return 0

    lax.fori_loop(0, tm, drain, 0, unroll=8)


def _dispatch(dest, h, xb_zero):
    T, D = h.shape
    tm = 256
    return pl.pallas_call(
        _dispatch_kernel,
        grid=(T // tm,),
        in_specs=[pl.BlockSpec((2 * tm,), lambda i: (i,), memory_space=pltpu.SMEM),
                  pl.BlockSpec((tm, D), lambda i: (i, 0)),
                  pl.BlockSpec(memory_space=pl.ANY)],
        out_specs=pl.BlockSpec(memory_space=pl.ANY),
        out_shape=jax.ShapeDtypeStruct(xb_zero.shape, F32),
        scratch_shapes=[pltpu.SemaphoreType.DMA(())],
        input_output_aliases={2: 0},
        compiler_params=_cparams(("arbitrary",)),
        name="moe_dispatch",
    )(dest, h, xb_zero)


def _expert_kernel(bexp_ref, nused_ref, x_ref, wg_ref, wu_ref, wd_ref, o_ref, wg_s, wu_s, wd_s):
    j = pl.program_id(0)
    used = j < nused_ref[0]
    new_expert = (j == 0) | (bexp_ref[j] != bexp_ref[jnp.maximum(j - 1, 0)])

    @pl.when(used & new_expert)
    def _():
        wg_s[...] = wg_ref[...].astype(BF16)
        wu_s[...] = wu_ref[...].astype(BF16)
        wd_s[...] = wd_ref[...].astype(BF16)

    @pl.when(used)
    def _():
        x = x_ref[...].astype(BF16)
        g = _dot(x, wg_s[...])
        u = _dot(x, wu_s[...])
        o_ref[...] = _dot((g * _sigmoid(g) * u).astype(BF16), wd_s[...])

    @pl.when(jnp.logical_not(used))
    def _():
        o_ref[...] = jnp.zeros(o_ref.shape, F32)


def _experts(bexp, nused, xb, w_gate, w_up, w_down, layer):
    cap, D = xb.shape
    nb = cap // MOE_ROWS
    F = D_EXPERT
    last = lambda j, be, nu: jnp.maximum(jnp.minimum(j, nu[0] - 1), 0)
    return pl.pallas_call(
        _expert_kernel,
        grid_spec=pltpu.PrefetchScalarGridSpec(
            num_scalar_prefetch=2,
            grid=(nb,),
            in_specs=[pl.BlockSpec((MOE_ROWS, D), lambda j, be, nu: (last(j, be, nu), 0)),
                      pl.BlockSpec((None, None, D, F), lambda j, be, nu: (layer, be[last(j, be, nu)], 0, 0)),
                      pl.BlockSpec((None, None, D, F), lambda j, be, nu: (layer, be[last(j, be, nu)], 0, 0)),
                      pl.BlockSpec((None, None, F, D), lambda j, be, nu: (layer, be[last(j, be, nu)], 0, 0))],
            out_specs=pl.BlockSpec((MOE_ROWS, D), lambda j, be, nu: (j, 0)),
            scratch_shapes=[pltpu.VMEM((D, F), BF16), pltpu.VMEM((D, F), BF16), pltpu.VMEM((F, D), BF16)],
        ),
        out_shape=jax.ShapeDtypeStruct((cap, D), F32),
        compiler_params=_cparams(("arbitrary",)),
        name="moe_experts",
    )(bexp, nused, xb, w_gate, w_up, w_down)


def _combine_kernel(dest_ref, x_ref, route_ref, g2_ref, fnw_ref, yb_ref, o_ref, gbuf, sem, *, final):
    tm = x_ref.shape[0]

    def issue(r, _):
        for k in range(2):
            _row_copy(yb_ref, gbuf.at[k], dest_ref[2 * r + k], r, sem).start()
        return 0

    lax.fori_loop(0, tm, issue, 0, unroll=8)

    def drain(r, _):
        for k in range(2):
            _row_copy(yb_ref, gbuf.at[k], 0, 0, sem).wait()
        return 0

    lax.fori_loop(0, tm, drain, 0, unroll=8)
    route = route_ref[...]
    y = route[:, 4:5] * gbuf[0] + route[:, 5:6] * gbuf[1]
    xo = x_ref[...] + g2_ref[...] * y
    if final:
        xo = _rms(xo, NORM_EPS) * fnw_ref[...]
    o_ref[...] = xo


def _combine(dest, xf, route, g2, fnw, yb, seq, final):
    T, D = xf.shape
    tm = 256
    per_b = seq // tm
    return pl.pallas_call(
        functools.partial(_combine_kernel, final=final),
        grid=(T // tm,),
        in_specs=[pl.BlockSpec((2 * tm,), lambda i: (i,), memory_space=pltpu.SMEM),
                  pl.BlockSpec((tm, D), lambda i: (i, 0)),
                  pl.BlockSpec((tm, LANES), lambda i: (i, 0)),
                  pl.BlockSpec((None, 1, D), lambda i: (i // per_b, 0, 0)),
                  pl.BlockSpec((1, D), lambda i: (0, 0)),
                  pl.BlockSpec(memory_space=pl.ANY)],
        out_specs=pl.BlockSpec((tm, D), lambda i: (i, 0)),
        out_shape=jax.ShapeDtypeStruct((T, D), F32),
        scratch_shapes=[pltpu.VMEM((2, tm, D), F32), pltpu.SemaphoreType.DMA(())],
        compiler_params=_cparams(("arbitrary",)),
        name="moe_combine",
    )(dest, xf, route, g2, fnw, yb)


def _rot_cols(w):
    half = w.shape[1] // 2
    return jnp.concatenate([-w[:, half:], w[:, :half]], axis=1)


def _pad_lanes(a, n=LANES):
    return jnp.pad(a, [(0, 0)] * (a.ndim - 1) + [(0, n - a.shape[-1])])


def kernel(x, c, positions, w_ada, b_ada, norm1_w, w_in, mla_q_norm_w, mla_w_uq, mla_kv_norm_w, mla_w_ukv, diff_lambda, diff_subln_w, ssd_conv_w, ssd_conv_b, ssd_dt_bias, ssd_a_log, ssd_d, ssd_norm_w, w_out, norm2_w, router_w_group, router_b_group, router_w_expert, router_b_expert, exp_w_gate, exp_w_up, exp_w_down, final_norm_w):
    B, S, D = x.shape
    T = B * S
    xf = x.reshape(T, D)
    mod = _ada(jnp.pad(c, ((0, SUBLANES - B), (0, 0))), w_ada, b_ada)[:, :B]

    pos_f = positions.astype(F32)
    pos_col = pos_f.reshape(T, 1)
    inv = 1.0 / (ROPE_THETA ** (jnp.arange(0, MLA_D_ROPE, 2, dtype=F32) / MLA_D_ROPE))
    inv = jnp.tile(inv, 4).reshape(1, LANES)
    slopes = jnp.exp2(-8.0 / DIFF_HEADS * jnp.arange(1, DIFF_HEADS + 1, dtype=F32)).reshape(DIFF_HEADS, 1, 1)
    slopes = slopes * LOG2E
    tq = min(FLASH_TQ, S)
    head_of_col = np.arange(SSD_D_INNER) // SSD_HEAD_DIM
    emat = jnp.asarray(np.arange(LANES)[:, None] == head_of_col[None, :], BF16)

    cap = T * 2 + N_EXPERTS * MOE_ROWS
    nb = cap // MOE_ROWS
    xb_zero = jnp.zeros((cap, D), F32)
    fnw = final_norm_w.reshape(1, D)

    for l in range(DEPTH):
        sh1, sc1, g1, sh2, sc2, g2 = [mod[l, :, i * D:(i + 1) * D].reshape(B, 1, D) for i in range(6)]
        w = w_in[l]
        ws = [jnp.concatenate([w[:, :_OFF[3]], _rot_cols(w[:, _OFF[2]:_OFF[3]])], axis=1)]
        ws.append(w[:, _OFF[3]:_OFF[4]] * (DIFF_D_QK ** -0.5 * LOG2E))
        ws += [w[:, _OFF[i]:_OFF[i + 1]] for i in range(4, 8)]
        ws.append(_pad_lanes(w[:, _OFF[8]:_OFF[9]]))
        ws = [a.astype(BF16) for a in ws]
        a_mla, dq, dk, dv, z, xbc, dt = _inproj(xf, norm1_w[l].reshape(1, D), sh1, sc1, ws, S)

        wq = mla_w_uq[l].reshape(MLA_Q_LORA, MLA_HEADS, MLA_D_NOPE + MLA_D_ROPE)
        wq_pe = wq[:, :, MLA_D_NOPE:]
        wq_rot = jnp.concatenate([-wq_pe[:, :, MLA_D_ROPE // 2:], wq_pe[:, :, :MLA_D_ROPE // 2]], axis=2)
        wq_cat = jnp.concatenate([wq, wq_rot], axis=2).reshape(MLA_Q_LORA, MLA_HEADS * 256).astype(BF16)
        q, k, v = _mla_prep(a_mla, pos_col, inv, mla_q_norm_w[l].reshape(1, -1), mla_kv_norm_w[l].reshape(1, -1),
                            wq_cat, mla_w_ukv[l].astype(BF16))
        q_t = q.transpose(0, 2, 1)
        v_t = v.reshape(MLA_HEADS, T // tq, tq, MLA_D_V).transpose(0, 1, 3, 2)
        y_mla = _mla_flash(q_t, k, v_t, B, S, tq)

        lam_init = 0.8 - 0.6 * math.exp(-0.3 * l)
        dv_t = dv.reshape(T // tq, tq, DIFF_HEADS * DIFF_D_V).transpose(0, 2, 1)
        y_diff = _diff_flash(dq.T, dk, dv_t, pos_f.reshape(B, 1, S), pos_f.reshape(T // tq, tq, 1), slopes,
                             diff_lambda[l], diff_subln_w[l].reshape(-1, 1), B, S, tq, lam_init)

        y_ssd = _ssd(xbc, z, dt, ssd_conv_w[l], ssd_conv_b[l].reshape(1, -1),
                     _pad_lanes(ssd_dt_bias[l].reshape(1, -1)), _pad_lanes(ssd_a_log[l].reshape(1, -1)),
                     jnp.repeat(ssd_d[l], SSD_HEAD_DIM).reshape(1, -1), ssd_norm_w[l].reshape(1, -1), emat, B, S)

        wo = w_out[l].astype(BF16)
        wr = _pad_lanes(jnp.concatenate([router_w_expert[l], router_w_group[l]], axis=1))
        wr_hi = wr.astype(BF16)
        wr = jnp.stack([wr_hi, (wr - wr_hi.astype(F32)).astype(BF16)])
        br = _pad_lanes(jnp.concatenate([router_b_expert[l], router_b_group[l]]).reshape(1, -1))
        x_mid, h2, route, cnt = _outproj_router(xf, y_mla, y_diff, y_ssd, g1, sh2, sc2, norm2_w[l].reshape(1, D),
                                                wo[:512], wo[512:1024], wo[1024:], wr, br, S)

        counts = cnt[0, :N_EXPERTS].astype(jnp.int32)
        pcounts = (counts + MOE_ROWS - 1) // MOE_ROWS * MOE_ROWS
        pend = jnp.cumsum(pcounts)
        pstart = pend - pcounts
        eid = route[:, 0:2].astype(jnp.int32)
        dest = (pstart[eid] + route[:, 2:4].astype(jnp.int32)).reshape(-1)
        starts = jnp.arange(nb, dtype=jnp.int32) * MOE_ROWS
        bexp = jnp.minimum(jnp.sum(pend[None, :] <= starts[:, None], axis=1), N_EXPERTS - 1).astype(jnp.int32)
        nused = (pend[-1:] // MOE_ROWS).astype(jnp.int32)

        xb = _dispatch(dest, h2, xb_zero)
        yb = _experts(bexp, nused, xb, exp_w_gate, exp_w_up, exp_w_down, l)
        xf = _combine(dest, x_mid, route, g2, fnw, yb, S, final=(l == DEPTH - 1))
    return xf.reshape(B, S, D)
```

```python
import functools
import math

import numpy as np
import jax
import jax.numpy as jnp
from jax import lax
from jax.experimental import pallas as pl
from jax.experimental.pallas import tpu as pltpu

F32 = jnp.float32
BF16 = jnp.bfloat16
HIGHEST = lax.Precision.HIGHEST

D_MODEL = 1024
DEPTH = 2
MLA_HEADS = 4
MLA_Q_LORA = 256
MLA_KV_LORA = 128
MLA_D_NOPE = 128
MLA_D_ROPE = 64
MLA_D_V = 128
ROPE_THETA = 10000.0
DIFF_HEADS = 4
DIFF_D_QK = 64
DIFF_D_V = 128
SSD_HEADS = 16
SSD_HEAD_DIM = 64
SSD_D_INNER = 1024
SSD_GROUPS = 4
SSD_STATE = 128
SSD_CONV = 4
SSD_CHUNK = 128
SSD_CONV_DIM = SSD_D_INNER + 2 * SSD_GROUPS * SSD_STATE
MOE_GROUPS = 4
MOE_EXPERTS_PER_GROUP = 8
N_EXPERTS = 32
D_EXPERT = 512
NORM_EPS = 1e-6
SUBLN_EPS = 1e-5

LANES = 128
SUBLANES = 8
MOE_ROWS = 256
FLASH_TQ = 1024
FLASH_TQ_DIFF = 512
LOG2E = math.log2(math.e)
VMEM_LIMIT = 48 * 1024 * 1024

_OFF = np.cumsum([0, MLA_Q_LORA, MLA_KV_LORA, MLA_D_ROPE, 512, 512, 512, SSD_D_INNER, SSD_CONV_DIM, SSD_HEADS])


def _cparams(sem):
    return pltpu.CompilerParams(dimension_semantics=sem, vmem_limit_bytes=VMEM_LIMIT)


def _dot(a, b, **kw):
    return jnp.dot(a, b, preferred_element_type=F32, **kw)


def _dot_nt(a, b):
    return lax.dot_general(a, b, (((1,), (1,)), ((), ())), preferred_element_type=F32)


def _rms(x, eps):
    return x * lax.rsqrt(jnp.mean(x * x, axis=-1, keepdims=True) + eps)


def _sigmoid(x):
    return 1.0 / (1.0 + jnp.exp(-x))


def _ada_kernel(c_ref, w_ref, b_ref, o_ref):
    c = c_ref[...]
    cond = c * _sigmoid(c)
    o_ref[...] = _dot(cond, w_ref[...], precision=HIGHEST) + b_ref[...]


def _ada(c_pad, w_ada, b_ada):
    L, D, N = w_ada.shape
    tn = 1024
    return pl.pallas_call(
        _ada_kernel,
        grid=(L, N // tn),
        in_specs=[
            pl.BlockSpec((SUBLANES, D), lambda l, j: (0, 0)),
            pl.BlockSpec((None, D, tn), lambda l, j: (l, 0, j)),
            pl.BlockSpec((None, 1, tn), lambda l, j: (l, 0, j)),
        ],
        out_specs=pl.BlockSpec((None, SUBLANES, tn), lambda l, j: (l, 0, j)),
        out_shape=jax.ShapeDtypeStruct((L, SUBLANES, N), F32),
        compiler_params=_cparams(("parallel", "parallel")),
        name="ada",
    )(c_pad, w_ada, b_ada.reshape(L, 1, N))


def _inproj_kernel(x_ref, nw_ref, sh_ref, sc_ref, wm_ref, wq_ref, wk_ref, wv_ref, wz_ref, wx_ref, wt_ref,
                   om_ref, oq_ref, ok_ref, ov_ref, oz_ref, ox_ref, ot_ref):
    x = x_ref[...]
    h = _rms(x, NORM_EPS) * nw_ref[...] * (1.0 + sc_ref[...]) + sh_ref[...]
    hb = h.astype(BF16)
    for w_ref, o_ref in ((wm_ref, om_ref), (wq_ref, oq_ref), (wk_ref, ok_ref), (wv_ref, ov_ref),
                         (wz_ref, oz_ref), (wx_ref, ox_ref), (wt_ref, ot_ref)):
        o_ref[...] = _dot(hb, w_ref[...]).astype(o_ref.dtype)


def _inproj(xf, nw, sh, sc, ws, seq):
    T, D = xf.shape
    tm = 256
    per_b = seq // tm
    widths = [w.shape[1] for w in ws]
    row = lambda i: (i, 0)
    const = lambda i: (0, 0)
    bat = lambda i: (i // per_b, 0, 0)
    in_specs = [pl.BlockSpec((tm, D), row), pl.BlockSpec((1, D), const),
                pl.BlockSpec((None, 1, D), bat), pl.BlockSpec((None, 1, D), bat)]
    in_specs += [pl.BlockSpec((D, n), const) for n in widths]
    out_dt = [BF16] * 6 + [F32]
    return pl.pallas_call(
        _inproj_kernel,
        grid=(T // tm,),
        in_specs=in_specs,
        out_specs=[pl.BlockSpec((tm, n), row) for n in widths],
        out_shape=[jax.ShapeDtypeStruct((T, n), dt) for n, dt in zip(widths, out_dt)],
        compiler_params=_cparams(("parallel",)),
        name="inproj",
    )(xf, nw, sh, sc, *ws)


def _mla_prep_kernel(a_ref, pos_ref, inv_ref, qnw_ref, kvnw_ref, wq_ref, wkv_ref, q_ref, k_ref, v_ref):
    a = a_ref[...].astype(F32)
    cqn = (_rms(a[:, :256], NORM_EPS) * qnw_ref[...]).astype(BF16)
    ckvn = (_rms(a[:, 256:384], NORM_EPS) * kvnw_ref[...]).astype(BF16)
    ang = pos_ref[...] * inv_ref[...]
    lane = lax.broadcasted_iota(jnp.int32, ang.shape, 1)
    cs = jnp.where(lane < MLA_D_ROPE, jnp.cos(ang), jnp.sin(ang))
    kp = a[:, 384:512] * cs
    krkr = kp + pltpu.roll(kp, MLA_D_ROPE, axis=1)
    scale = (MLA_D_NOPE + MLA_D_ROPE) ** -0.5 * LOG2E
    qall = _dot(cqn, wq_ref[...]) * scale
    kvall = _dot(ckvn, wkv_ref[...])
    for h in range(MLA_HEADS):
        o = h * 256
        q_ref[h] = jnp.concatenate([qall[:, o:o + 128], qall[:, o + 128:o + 256] * cs], axis=1).astype(BF16)
        k_ref[h] = jnp.concatenate([kvall[:, o:o + 128], krkr], axis=1).astype(BF16)
        v_ref[h] = kvall[:, o + 128:o + 256].astype(BF16)


def _mla_prep(a, pos_col, inv, qnw, kvnw, wq, wkv):
    T = a.shape[0]
    tm = 512
    row = lambda i: (i, 0)
    const = lambda i: (0, 0)
    hrow = lambda i: (0, i, 0)
    return pl.pallas_call(
        _mla_prep_kernel,
        grid=(T // tm,),
        in_specs=[pl.BlockSpec((tm, 512), row), pl.BlockSpec((tm, 1), row), pl.BlockSpec((1, LANES), const),
                  pl.BlockSpec((1, 256), const), pl.BlockSpec((1, 128), const),
                  pl.BlockSpec((256, 1024), const), pl.BlockSpec((128, 1024), const)],
        out_specs=[pl.BlockSpec((MLA_HEADS, tm, 256), hrow), pl.BlockSpec((MLA_HEADS, tm, 256), hrow),
                   pl.BlockSpec((MLA_HEADS, tm, 128), hrow)],
        out_shape=[jax.ShapeDtypeStruct((MLA_HEADS, T, 256), BF16), jax.ShapeDtypeStruct((MLA_HEADS, T, 256), BF16),
                   jax.ShapeDtypeStruct((MLA_HEADS, T, 128), BF16)],
        compiler_params=_cparams(("parallel",)),
        name="mla_prep",
    )(a, pos_col, inv, qnw, kvnw, wq, wkv)


FLASH_STRIP = LANES


def _flash_init(m_s, l_s, acc_s, p_s, a_s, s_s):
    del s_s
    m_s[...] = jnp.full(m_s.shape, -jnp.inf, F32)
    l_s[...] = jnp.zeros(l_s.shape, F32)
    acc_s[...] = jnp.zeros(acc_s.shape, F32)
    p_s[1] = jnp.zeros(p_s.shape[1:], BF16)
    a_s[1] = jnp.ones(a_s.shape[1:], F32)


def _flash_pv(v_t, slot, acc_s, p_s, a_s):
    acc_s[...] = a_s[slot] * acc_s[...] + _dot(v_t, p_s[slot])


def _flash_stage(next_scores, strip_fn, v_prev, cur, m_s, l_s, acc_s, p_s, a_s, s_s):
    if next_scores is not None:
        s_s[1 - cur] = next_scores()
    _flash_pv(v_prev, 1 - cur, acc_s, p_s, a_s)
    for j in range(m_s.shape[1] // FLASH_STRIP):
        sl = slice(j * FLASH_STRIP, (j + 1) * FLASH_STRIP)
        s, bias, visible = strip_fn(s_s, cur, j)
        keep = (lambda x: x) if visible is None else (lambda x: jnp.where(visible, x, -jnp.inf))
        m_old = m_s[:, sl]
        t = s if bias is None else s - bias
        m_new = jnp.maximum(m_old, jnp.max(keep(t), axis=0, keepdims=True))
        alpha = jnp.exp2(m_old - m_new)
        p = jnp.exp2(keep(t - m_new))
        l_s[:, sl] = alpha * l_s[:, sl] + jnp.sum(p, axis=0, keepdims=True)
        m_s[:, sl] = m_new
        p_s[cur, :, sl] = p.astype(BF16)
        a_s[cur, :, sl] = alpha


def _flash_run(qi, scores, stage, flush, s_s):
    if scores is not None:
        s_s[0] = scores(0)

    def pair(i, _):
        a = 2 * i
        stage(a + 1, a, False, 0)
        stage(a + 2, a + 1, False, 1)
        return 0

    lax.fori_loop(0, qi // 2, pair, 0)

    @pl.when(qi % 2 == 0)
    def _():
        stage(None, qi, True, 0)
        flush(qi, 0)

    @pl.when(qi % 2 == 1)
    def _():
        stage(qi, qi - 1, False, 0)
        stage(None, qi, True, 1)
        flush(qi, 1)


def _causal_strip(shape, j):
    r = lax.broadcasted_iota(jnp.int32, shape, 0)
    c = lax.broadcasted_iota(jnp.int32, shape, 1) + j * FLASH_STRIP
    return r <= c


def _flash_scratch(dv, tk, nq, score_slots=True):
    s_shape = (2, tk, nq) if score_slots else (2, SUBLANES, LANES)
    return [pltpu.VMEM((1, nq), F32), pltpu.VMEM((1, nq), F32), pltpu.VMEM((dv, nq), F32),
            pltpu.VMEM((2, tk, nq), BF16), pltpu.VMEM((2, 1, nq), F32), pltpu.VMEM(s_shape, F32)]


def _mla_flash_kernel(qt_ref, k_ref, vt_ref, o_ref, *state, tq):
    m_s, l_s, acc_s, p_s, a_s, s_s = state
    qi = pl.program_id(2)
    _flash_init(*state)
    q_t = qt_ref[...]

    def scores(kb):
        return _dot(k_ref[pl.ds(pl.multiple_of(kb * tq, tq), tq), :], q_t)

    def stage(nxt, kb, masked, cur):
        def strip(s_ref, slot, j):
            t = s_ref[slot, :, j * FLASH_STRIP:(j + 1) * FLASH_STRIP]
            return t, None, (_causal_strip(t.shape, j) if masked else None)

        _flash_stage(None if nxt is None else (lambda: scores(nxt)), strip, vt_ref[jnp.maximum(kb - 1, 0)], cur,
                     *state)

    _flash_run(qi, scores, stage, lambda kb, slot: _flash_pv(vt_ref[kb], slot, acc_s, p_s, a_s), s_s)
    o_ref[...] = (acc_s[...] * (1.0 / l_s[...])).T.astype(o_ref.dtype)


def _mla_flash(q_t, k, v_t, batch, seq, tq):
    nq = seq // tq
    return pl.pallas_call(
        functools.partial(_mla_flash_kernel, tq=tq),
        grid=(batch, MLA_HEADS, nq),
        in_specs=[pl.BlockSpec((None, 256, tq), lambda b, h, i: (h, 0, b * nq + i)),
                  pl.BlockSpec((None, seq, 256), lambda b, h, i: (h, b, 0)),
                  pl.BlockSpec((None, nq, MLA_D_V, tq), lambda b, h, i: (h, b, 0, 0))],
        out_specs=pl.BlockSpec((tq, MLA_D_V), lambda b, h, i: (b * nq + i, h)),
        out_shape=jax.ShapeDtypeStruct((batch * seq, MLA_HEADS * MLA_D_V), BF16),
        scratch_shapes=_flash_scratch(MLA_D_V, tq, tq),
        compiler_params=_cparams(("parallel", "parallel", "arbitrary")),
        name="mla_flash",
    )(q_t, k, v_t)


def _diff_flash_kernel(qt_ref, k_ref, vt_ref, pq_ref, pk_ref, slope_ref, lam_ref, sw_ref, o_ref, *state,
                       tq, lam_init, ordered):
    m_s, l_s, acc_s, p_s, a_s, s_s = state
    qi = pl.program_id(2)
    _flash_init(*state)
    q_t = qt_ref[...]
    row = lax.broadcasted_iota(jnp.int32, q_t.shape, 0)
    zero = jnp.zeros_like(q_t)
    q2_t = jnp.concatenate([jnp.where(row < DIFF_D_QK, q_t, zero), jnp.where(row >= DIFF_D_QK, q_t, zero)], axis=1)
    spm = tq // FLASH_STRIP
    if ordered:
        ones = jnp.where(lax.broadcasted_iota(jnp.int32, q2_t.shape, 0) < 3, 1.0, 0.0).astype(BF16)
        q2_t = jnp.concatenate([q2_t, ones], axis=0)
    else:
        slope = slope_ref[...]
        spq = slope * pq_ref[...]

    def scores(kb):
        return _dot(k_ref[pl.ds(pl.multiple_of(kb * tq, tq), tq), :], q2_t)

    def stage(nxt, kb, masked, cur):
        del nxt
        s_t = scores(kb)
        if not ordered:
            bias = jnp.abs(slope * pk_ref[kb] - spq)

        def strip(s_ref, slot, j):
            jq = j % spm
            t = s_t[:, j * FLASH_STRIP:(j + 1) * FLASH_STRIP]
            return (t, None if ordered else bias[:, jq * FLASH_STRIP:(jq + 1) * FLASH_STRIP],
                    _causal_strip(t.shape, jq) if masked else None)

        _flash_stage(None, strip, vt_ref[jnp.maximum(kb - 1, 0)], cur, *state)

    _flash_run(qi, None, stage, lambda kb, slot: _flash_pv(vt_ref[kb], slot, acc_s, p_s, a_s), s_s)
    o2 = acc_s[...] * (1.0 / l_s[...])
    lf = lam_ref[...]
    lam = (jnp.exp(jnp.sum(lf[0:1] * lf[1:2], keepdims=True))
           - jnp.exp(jnp.sum(lf[2:3] * lf[3:4], keepdims=True)) + lam_init)
    out = o2[:, :tq] - lam * o2[:, tq:]
    out = out * lax.rsqrt(jnp.mean(out * out, axis=0, keepdims=True) + SUBLN_EPS) * sw_ref[...] * (1.0 - lam_init)
    o_ref[...] = out.T.astype(o_ref.dtype)


def _diff_flash(q_t, k, v_t, pos_row, pos_kcol, slopes, lam, sw_col, batch, seq, tq, lam_init, ordered):
    nq = seq // tq
    return pl.pallas_call(
        functools.partial(_diff_flash_kernel, tq=tq, lam_init=lam_init, ordered=ordered),
        grid=(batch, DIFF_HEADS, nq),
        in_specs=[pl.BlockSpec((128, tq), lambda b, h, i: (h, b * nq + i)),
                  pl.BlockSpec((seq, k.shape[1] // DIFF_HEADS), lambda b, h, i: (b, h)),
                  pl.BlockSpec((nq, DIFF_D_V, tq), lambda b, h, i: (b, h, 0)),
                  pl.BlockSpec((None, 1, tq), lambda b, h, i: (b, 0, i)),
                  pl.BlockSpec((nq, tq, 1), lambda b, h, i: (b, 0, 0)),
                  pl.BlockSpec((None, 1, 1), lambda b, h, i: (h, 0, 0)),
                  pl.BlockSpec((4, DIFF_D_QK), lambda b, h, i: (0, 0)),
                  pl.BlockSpec((DIFF_D_V, 1), lambda b, h, i: (0, 0))],
        out_specs=pl.BlockSpec((tq, DIFF_D_V), lambda b, h, i: (b * nq + i, h)),
        out_shape=jax.ShapeDtypeStruct((batch * seq, DIFF_HEADS * DIFF_D_V), BF16),
        scratch_shapes=_flash_scratch(DIFF_D_V, tq, 2 * tq, score_slots=False),
        compiler_params=_cparams(("parallel", "parallel", "arbitrary")),
        name="diff_flash",
    )(q_t, k, v_t, pos_row, pos_kcol, slopes, lam, sw_col)


def _ssd_kernel(xbc_ref, z_ref, dt_ref, cw_ref, cb_ref, dtb_ref, alog_ref, dsk_ref, nw_ref, e_ref, o_ref,
                ext, state):
    L = SSD_CHUNK
    G, N = SSD_GROUPS, SSD_STATE
    GW = SSD_D_INNER // G

    @pl.when(pl.program_id(1) == 0)
    def _():
        ext[0:SUBLANES, :] = jnp.zeros((SUBLANES, SSD_CONV_DIM), F32)
        state[...] = jnp.zeros(state.shape, F32)

    u = xbc_ref[...].astype(F32)
    ext[SUBLANES:SUBLANES + L, :] = u
    acc = cb_ref[...] + cw_ref[SSD_CONV - 1:SSD_CONV, :] * u
    for k in range(SSD_CONV - 1):
        o = SUBLANES - (SSD_CONV - 1) + k
        acc = acc + cw_ref[k:k + 1, :] * ext[o:o + L, :]
    ext[0:SUBLANES, :] = u[L - SUBLANES:L, :]
    act = acc * _sigmoid(acc)
    xs = act[:, :SSD_D_INNER]
    bm = act[:, SSD_D_INNER:SSD_D_INNER + G * N]
    cm = act[:, SSD_D_INNER + G * N:]

    dtr = dt_ref[...] + dtb_ref[...]
    dtv = jnp.maximum(dtr, 0.0) + jnp.log1p(jnp.exp(-jnp.abs(dtr)))
    adt = dtv * (-jnp.exp(alog_ref[...]))
    r = lax.broadcasted_iota(jnp.int32, (L, L), 0)
    c = lax.broadcasted_iota(jnp.int32, (L, L), 1)
    tril = r >= c
    a_cs = _dot(tril.astype(F32), adt, precision=HIGHEST)
    a_cs_t = a_cs.T
    ea = jnp.exp(a_cs)
    dte = jnp.exp(a_cs[L - 1:L, :] - a_cs)
    stack = jnp.concatenate([dtv, dte, ea], axis=0)
    hi, lo = _split_bf16(stack, 2)
    ex = _dot(hi, e_ref[...]) + _dot(lo, e_ref[...])
    dt_e, dte_e, ea_e = ex[:L], ex[L:2 * L], ex[2 * L:]
    xdt = xs * dt_e
    xdt_b = xdt.astype(BF16)
    xw_b = (xdt * dte_e).astype(BF16)
    bb = bm.astype(BF16)
    cbf = cm.astype(BF16)
    lane_g = lax.broadcasted_iota(jnp.int32, (1, GW), 1)
    ys = []
    for g in range(G):
        bg = bb[:, g * N:(g + 1) * N]
        cg = cbf[:, g * N:(g + 1) * N]
        cb = _dot_nt(cg, bg)
        xg = xdt_b[:, g * GW:(g + 1) * GW]
        yd = jnp.zeros((L, GW), F32)
        for rr in range(SSD_HEADS // G):
            h = g * (SSD_HEADS // G) + rr
            seg = a_cs[:, h:h + 1] - a_cs_t[h:h + 1, :]
            dec = jnp.exp(jnp.where(tril, seg, -jnp.inf))
            in_head = (lane_g >= rr * SSD_HEAD_DIM) & (lane_g < (rr + 1) * SSD_HEAD_DIM)
            yd = yd + _dot((cb * dec).astype(BF16), jnp.where(in_head, xg, jnp.zeros_like(xg)))
        st_old = state[g]
        eg = ea_e[:, g * GW:(g + 1) * GW]
        yoff = _dot(cg, st_old.astype(BF16)) * eg
        bg_t = bm[:, g * N:(g + 1) * N].T.astype(BF16)
        state[g] = st_old * eg[L - 1:L, :] + _dot(bg_t, xw_b[:, g * GW:(g + 1) * GW])
        ys.append(yd + yoff)
    y = jnp.concatenate(ys, axis=1) + xs * dsk_ref[...]
    zf = z_ref[...].astype(F32)
    y = y * (zf * _sigmoid(zf))
    y = jnp.concatenate([_rms(y[:, g * GW:(g + 1) * GW], NORM_EPS) for g in range(G)], axis=1)
    o_ref[...] = (y * nw_ref[...]).astype(o_ref.dtype)


def _ssd(xbc, z, dt, cw, cb, dtb, alog, dsk, nw, emat, batch, seq):
    L = SSD_CHUNK
    nc = seq // L
    row = lambda b, c: (b * nc + c, 0)
    const = lambda b, c: (0, 0)
    return pl.pallas_call(
        _ssd_kernel,
        grid=(batch, nc),
        in_specs=[pl.BlockSpec((L, SSD_CONV_DIM), row), pl.BlockSpec((L, SSD_D_INNER), row),
                  pl.BlockSpec((L, LANES), row),
                  pl.BlockSpec((SSD_CONV, SSD_CONV_DIM), const), pl.BlockSpec((1, SSD_CONV_DIM), const),
                  pl.BlockSpec((1, LANES), const), pl.BlockSpec((1, LANES), const),
                  pl.BlockSpec((1, SSD_D_INNER), const), pl.BlockSpec((1, SSD_D_INNER), const),
                  pl.BlockSpec((LANES, SSD_D_INNER), const)],
        out_specs=pl.BlockSpec((L, SSD_D_INNER), row),
        out_shape=jax.ShapeDtypeStruct((batch * seq, SSD_D_INNER), BF16),
        scratch_shapes=[pltpu.VMEM((SUBLANES + L, SSD_CONV_DIM), F32),
                        pltpu.VMEM((SSD_GROUPS, SSD_STATE, SSD_D_INNER // SSD_GROUPS), F32)],
        compiler_params=_cparams(("arbitrary", "arbitrary")),
        name="ssd",
    )(xbc, z, dt, cw, cb, dtb, alog, dsk, nw, emat)


def _outproj_router_kernel(x_ref, ym_ref, yd_ref, ys_ref, g1_ref, sh_ref, sc_ref, nw_ref, wm_ref, wd_ref, ws_ref,
                           wr_ref, br_ref, xo_ref, h_ref, route_ref, cnt_ref, carry):
    @pl.when(pl.program_id(0) == 0)
    def _():
        carry[...] = jnp.zeros(carry.shape, F32)

    y = _dot(ym_ref[...], wm_ref[...]) + _dot(yd_ref[...], wd_ref[...]) + _dot(ys_ref[...], ws_ref[...])
    xn = x_ref[...] + g1_ref[...] * y
    xo_ref[...] = xn
    h = _rms(xn, NORM_EPS) * nw_ref[...] * (1.0 + sc_ref[...]) + sh_ref[...]
    h_ref[...] = h
    h_hi, h_lo = _split_bf16(h, 2)
    logits = _dot(h_hi, wr_ref[0]) + (_dot(h_lo, wr_ref[0]) + _dot(h_hi, wr_ref[1])) + br_ref[...]
    tm = logits.shape[0]
    lane = lax.broadcasted_iota(jnp.int32, logits.shape, 1)
    big = jnp.int32(1 << 20)
    neg = -jnp.inf
    lg = jnp.where((lane >= N_EXPERTS) & (lane < N_EXPERTS + MOE_GROUPS), logits, neg)
    mg = jnp.max(lg, axis=-1, keepdims=True)
    pg_top = 1.0 / jnp.sum(jnp.exp(lg - mg), axis=-1, keepdims=True)
    gsel = jnp.min(jnp.where(lg == mg, lane, big), axis=-1, keepdims=True) - N_EXPERTS
    le = jnp.where((lane < N_EXPERTS) & ((lane // MOE_EXPERTS_PER_GROUP) == gsel), logits, neg)
    m1 = jnp.max(le, axis=-1, keepdims=True)
    e1 = jnp.min(jnp.where(le == m1, lane, big), axis=-1, keepdims=True)
    le2 = jnp.where(lane == e1, neg, le)
    m2 = jnp.max(le2, axis=-1, keepdims=True)
    e2 = jnp.min(jnp.where(le2 == m2, lane, big), axis=-1, keepdims=True)
    rr = jnp.exp(m2 - m1)
    gate1 = pg_top / (1.0 + rr)
    gate2 = pg_top * rr / (1.0 + rr)
    onehot = jnp.where((lane == e1) | (lane == e2), 1.0, 0.0)
    ri = lax.broadcasted_iota(jnp.int32, (tm, tm), 0)
    ci = lax.broadcasted_iota(jnp.int32, (tm, tm), 1)
    before = jnp.where(ri > ci, 1.0, 0.0).astype(BF16)
    prefix = _dot(before, onehot.astype(BF16)) + carry[...]
    rank1 = jnp.sum(jnp.where(lane == e1, prefix, 0.0), axis=-1, keepdims=True)
    rank2 = jnp.sum(jnp.where(lane == e2, prefix, 0.0), axis=-1, keepdims=True)
    carry[...] = carry[...] + jnp.sum(onehot, axis=0, keepdims=True)
    cnt_ref[...] = carry[...]
    cols = (e1.astype(F32), e2.astype(F32), rank1, rank2, gate1, gate2)
    route = jnp.zeros(logits.shape, F32)
    for i, col in enumerate(cols):
        route = jnp.where(lane == i, col, route)
    route_ref[...] = route


def _outproj_router(xf, ym, yd, ys, g1, sh2, sc2, n2w, wom, wod, wos, wr, br, seq):
    T, D = xf.shape
    tm = 256
    per_b = seq // tm
    row = lambda i: (i, 0)
    const = lambda i: (0, 0)
    bat = lambda i: (i // per_b, 0, 0)
    return pl.pallas_call(
        _outproj_router_kernel,
        grid=(T // tm,),
        in_specs=[pl.BlockSpec((tm, D), row), pl.BlockSpec((tm, 512), row), pl.BlockSpec((tm, 512), row),
                  pl.BlockSpec((tm, SSD_D_INNER), row),
                  pl.BlockSpec((None, 1, D), bat), pl.BlockSpec((None, 1, D), bat), pl.BlockSpec((None, 1, D), bat),
                  pl.BlockSpec((1, D), const),
                  pl.BlockSpec((512, D), const), pl.BlockSpec((512, D), const), pl.BlockSpec((SSD_D_INNER, D), const),
                  pl.BlockSpec((2, D, LANES), lambda i: (0, 0, 0)), pl.BlockSpec((1, LANES), const)],
        out_specs=[pl.BlockSpec((tm, D), row), pl.BlockSpec((tm, D), row), pl.BlockSpec((tm, LANES), row),
                   pl.BlockSpec((1, LANES), const)],
        out_shape=[jax.ShapeDtypeStruct((T, D), F32), jax.ShapeDtypeStruct((T, D), F32),
                   jax.ShapeDtypeStruct((T, LANES), F32), jax.ShapeDtypeStruct((1, LANES), F32)],
        scratch_shapes=[pltpu.VMEM((1, LANES), F32)],
        compiler_params=_cparams(("arbitrary",)),
        name="outproj_router",
    )(xf, ym, yd, ys, g1, sh2, sc2, n2w, wom, wod, wos, wr, br)


def _row_copy(src, dst, i_src, i_dst, sem):
    return pltpu.make_async_copy(src.at[pl.ds(i_src, 1)], dst.at[pl.ds(i_dst, 1)], sem)


def _dispatch_kernel(dest_ref, h_ref, xb_in_ref, xb_ref, sem):
    del xb_in_ref
    tm = h_ref.shape[0]

    def issue(r, _):
        for k in range(2):
            _row_copy(h_ref, xb_ref, r, dest_ref[2 * r + k], sem).start()
        return 0

    lax.fori_loop(0, tm, issue, 0, unroll=8)

    def drain(r, _):
        for k in range(2):
            _row_copy(h_ref, xb_ref, 0, 0, sem).wait()
        return 0

    lax.fori_loop(0, tm, drain, 0, unroll=8)


def _dispatch(dest, h, xb_zero):
    T, D = h.shape
    tm = 256
    return pl.pallas_call(
        _dispatch_kernel,
        grid=(T // tm,),
        in_specs=[pl.BlockSpec((2 * tm,), lambda i: (i,), memory_space=pltpu.SMEM),
                  pl.BlockSpec((tm, D), lambda i: (i, 0)),
                  pl.BlockSpec(memory_space=pl.ANY)],
        out_specs=pl.BlockSpec(memory_space=pl.ANY),
        out_shape=jax.ShapeDtypeStruct(xb_zero.shape, F32),
        scratch_shapes=[pltpu.SemaphoreType.DMA(())],
        input_output_aliases={2: 0},
        compiler_params=_cparams(("arbitrary",)),
        name="moe_dispatch",
    )(dest, h, xb_zero)


def _expert_kernel(bexp_ref, nused_ref, x_ref, wg_ref, wu_ref, wd_ref, o_ref, wg_s, wu_s, wd_s):
    j = pl.program_id(0)
    used = j < nused_ref[0]
    new_expert = (j == 0) | (bexp_ref[j] != bexp_ref[jnp.maximum(j - 1, 0)])

    @pl.when(used & new_expert)
    def _():
        wg_s[...] = wg_ref[...].astype(BF16)
        wu_s[...] = wu_ref[...].astype(BF16)
        wd_s[...] = wd_ref[...].astype(BF16)

    @pl.when(used)
    def _():
        x = x_ref[...].astype(BF16)
        g = _dot(x, wg_s[...])
        u = _dot(x, wu_s[...])
        o_ref[...] = _dot((g * _sigmoid(g) * u).astype(BF16), wd_s[...])

    @pl.when(jnp.logical_not(used))
    def _():
        o_ref[...] = jnp.zeros(o_ref.shape, F32)


def _experts(bexp, nused, xb, w_gate, w_up, w_down, layer):
    cap, D = xb.shape
    nb = cap // MOE_ROWS
    F = D_EXPERT
    last = lambda j, be, nu: jnp.maximum(jnp.minimum(j, nu[0] - 1), 0)
    return pl.pallas_call(
        _expert_kernel,
        grid_spec=pltpu.PrefetchScalarGridSpec(
            num_scalar_prefetch=2,
            grid=(nb,),
            in_specs=[pl.BlockSpec((MOE_ROWS, D), lambda j, be, nu: (last(j, be, nu), 0)),
                      pl.BlockSpec((None, None, D, F), lambda j, be, nu: (layer, be[last(j, be, nu)], 0, 0)),
                      pl.BlockSpec((None, None, D, F), lambda j, be, nu: (layer, be[last(j, be, nu)], 0, 0)),
                      pl.BlockSpec((None, None, F, D), lambda j, be, nu: (layer, be[last(j, be, nu)], 0, 0))],
            out_specs=pl.BlockSpec((MOE_ROWS, D), lambda j, be, nu: (j, 0)),
            scratch_shapes=[pltpu.VMEM((D, F), BF16), pltpu.VMEM((D, F), BF16), pltpu.VMEM((F, D), BF16)],
        ),
        out_shape=jax.ShapeDtypeStruct((cap, D), F32),
        compiler_params=_cparams(("arbitrary",)),
        name="moe_experts",
    )(bexp, nused, xb, w_gate, w_up, w_down)


def _combine_kernel(dest_ref, x_ref, route_ref, g2_ref, fnw_ref, yb_ref, o_ref, gbuf, sem, *, final):
    tm = x_ref.shape[0]

    def issue(r, _):
        for k in range(2):
            _row_copy(yb_ref, gbuf.at[k], dest_ref[2 * r + k], r, sem).start()
        return 0

    lax.fori_loop(0, tm, issue, 0, unroll=8)

    def drain(r, _):
        for k in range(2):
            _row_copy(yb_ref, gbuf.at[k], 0, 0, sem).wait()
        return 0

    lax.fori_loop(0, tm, drain, 0, unroll=8)
    route = route_ref[...]
    y = route[:, 4:5] * gbuf[0] + route[:, 5:6] * gbuf[1]
    xo = x_ref[...] + g2_ref[...] * y
    if final:
        xo = _rms(xo, NORM_EPS) * fnw_ref[...]
    o_ref[...] = xo


def _combine(dest, xf, route, g2, fnw, yb, seq, final):
    T, D = xf.shape
    tm = 256
    per_b = seq // tm
    return pl.pallas_call(
        functools.partial(_combine_kernel, final=final),
        grid=(T // tm,),
        in_specs=[pl.BlockSpec((2 * tm,), lambda i: (i,), memory_space=pltpu.SMEM),
                  pl.BlockSpec((tm, D), lambda i: (i, 0)),
                  pl.BlockSpec((tm, LANES), lambda i: (i, 0)),
                  pl.BlockSpec((None, 1, D), lambda i: (i // per_b, 0, 0)),
                  pl.BlockSpec((1, D), lambda i: (0, 0)),
                  pl.BlockSpec(memory_space=pl.ANY)],
        out_specs=pl.BlockSpec((tm, D), lambda i: (i, 0)),
        out_shape=jax.ShapeDtypeStruct((T, D), F32),
        scratch_shapes=[pltpu.VMEM((2, tm, D), F32), pltpu.SemaphoreType.DMA(())],
        compiler_params=_cparams(("arbitrary",)),
        name="moe_combine",
    )(dest, xf, route, g2, fnw, yb)


def _rot_cols(w):
    half = w.shape[1] // 2
    return jnp.concatenate([-w[:, half:], w[:, :half]], axis=1)


def _split_bf16(x, n):
    pieces = []
    for _ in range(n):
        head = lax.bitcast_convert_type(lax.bitcast_convert_type(x, jnp.uint32) & jnp.uint32(0xFFFF0000), F32)
        pieces.append(head.astype(BF16))
        x = x - head
    return pieces


def _pad_lanes(a, n=LANES):
    return jnp.pad(a, [(0, 0)] * (a.ndim - 1) + [(0, n - a.shape[-1])])


def kernel(x, c, positions, w_ada, b_ada, norm1_w, w_in, mla_q_norm_w, mla_w_uq, mla_kv_norm_w, mla_w_ukv, diff_lambda, diff_subln_w, ssd_conv_w, ssd_conv_b, ssd_dt_bias, ssd_a_log, ssd_d, ssd_norm_w, w_out, norm2_w, router_w_group, router_b_group, router_w_expert, router_b_expert, exp_w_gate, exp_w_up, exp_w_down, final_norm_w):
    B, S, D = x.shape
    T = B * S
    xf = x.reshape(T, D)
    mod = _ada(jnp.pad(c, ((0, SUBLANES - B), (0, 0))), w_ada, b_ada)[:, :B]

    pos_f = positions.astype(F32)
    pos_col = pos_f.reshape(T, 1)
    inv = 1.0 / (ROPE_THETA ** (jnp.arange(0, MLA_D_ROPE, 2, dtype=F32) / MLA_D_ROPE))
    inv = jnp.tile(inv, 4).reshape(1, LANES)
    slopes = jnp.exp2(-8.0 / DIFF_HEADS * jnp.arange(1, DIFF_HEADS + 1, dtype=F32)).reshape(DIFF_HEADS, 1, 1)
    slopes = slopes * LOG2E
    tq = min(FLASH_TQ, S)
    tqd = min(FLASH_TQ_DIFF, S)
    pos_ordered = jnp.all(positions[:, 1:] >= positions[:, :-1])
    head_of_col = np.arange(SSD_D_INNER) // SSD_HEAD_DIM
    emat = jnp.asarray(np.arange(LANES)[:, None] == head_of_col[None, :], BF16)

    cap = T * 2 + N_EXPERTS * MOE_ROWS
    nb = cap // MOE_ROWS
    xb_zero = jnp.zeros((cap, D), F32)
    fnw = final_norm_w.reshape(1, D)

    for l in range(DEPTH):
        sh1, sc1, g1, sh2, sc2, g2 = [mod[l, :, i * D:(i + 1) * D].reshape(B, 1, D) for i in range(6)]
        w = w_in[l]
        ws = [jnp.concatenate([w[:, :_OFF[3]], _rot_cols(w[:, _OFF[2]:_OFF[3]])], axis=1)]
        ws.append(w[:, _OFF[3]:_OFF[4]] * (DIFF_D_QK ** -0.5 * LOG2E))
        ws += [w[:, _OFF[i]:_OFF[i + 1]] for i in range(4, 8)]
        ws.append(_pad_lanes(w[:, _OFF[8]:_OFF[9]]))
        ws = [a.astype(BF16) for a in ws]
        a_mla, dq, dk, dv, z, xbc, dt = _inproj(xf, norm1_w[l].reshape(1, D), sh1, sc1, ws, S)

        wq = mla_w_uq[l].reshape(MLA_Q_LORA, MLA_HEADS, MLA_D_NOPE + MLA_D_ROPE)
        wq_pe = wq[:, :, MLA_D_NOPE:]
        wq_rot = jnp.concatenate([-wq_pe[:, :, MLA_D_ROPE // 2:], wq_pe[:, :, :MLA_D_ROPE // 2]], axis=2)
        wq_cat = jnp.concatenate([wq, wq_rot], axis=2).reshape(MLA_Q_LORA, MLA_HEADS * 256).astype(BF16)
        q, k, v = _mla_prep(a_mla, pos_col, inv, mla_q_norm_w[l].reshape(1, -1), mla_kv_norm_w[l].reshape(1, -1),
                            wq_cat, mla_w_ukv[l].astype(BF16))
        q_t = q.transpose(0, 2, 1)
        v_t = v.reshape(MLA_HEADS, T // tq, tq, MLA_D_V).transpose(0, 1, 3, 2)
        y_mla = _mla_flash(q_t, k, v_t, B, S, tq)

        lam_init = 0.8 - 0.6 * math.exp(-0.3 * l)
        dv_t = dv.reshape(T // tqd, tqd, DIFF_HEADS * DIFF_D_V).transpose(0, 2, 1)
        diff_args = (dq.T, dv_t, diff_lambda[l], diff_subln_w[l].reshape(-1, 1))

        def diff_any_positions(dk_, args):
            return _diff_flash(args[0], dk_, args[1], pos_f.reshape(B, 1, S), pos_f.reshape(T // tqd, tqd, 1),
                               slopes, args[2], args[3], B, S, tqd, lam_init, ordered=False)

        def diff_ordered_positions(dk_, args):
            spk = pos_col * slopes.reshape(1, DIFF_HEADS)
            pieces = _pad_lanes(jnp.stack(_split_bf16(spk, 3), axis=-1))
            k_aug = jnp.concatenate([dk_.reshape(T, DIFF_HEADS, 2 * DIFF_D_QK), pieces], axis=-1)
            return _diff_flash(args[0], k_aug.reshape(T, DIFF_HEADS * 256), args[1], pos_f.reshape(B, 1, S),
                               pos_f.reshape(T // tqd, tqd, 1), slopes, args[2], args[3], B, S, tqd, lam_init,
                               ordered=True)

        y_diff = lax.cond(pos_ordered, diff_ordered_positions, diff_any_positions, dk, diff_args)

        y_ssd = _ssd(xbc, z, dt, ssd_conv_w[l], ssd_conv_b[l].reshape(1, -1),
                     _pad_lanes(ssd_dt_bias[l].reshape(1, -1)), _pad_lanes(ssd_a_log[l].reshape(1, -1)),
                     jnp.repeat(ssd_d[l], SSD_HEAD_DIM).reshape(1, -1), ssd_norm_w[l].reshape(1, -1), emat, B, S)

        wo = w_out[l].astype(BF16)
        wr = _pad_lanes(jnp.concatenate([router_w_expert[l], router_w_group[l]], axis=1))
        wr = jnp.stack(_split_bf16(wr, 2))
        br = _pad_lanes(jnp.concatenate([router_b_expert[l], router_b_group[l]]).reshape(1, -1))
        x_mid, h2, route, cnt = _outproj_router(xf, y_mla, y_diff, y_ssd, g1, sh2, sc2, norm2_w[l].reshape(1, D),
                                                wo[:512], wo[512:1024], wo[1024:], wr, br, S)

        counts = cnt[0, :N_EXPERTS].astype(jnp.int32)
        pcounts = (counts + MOE_ROWS - 1) // MOE_ROWS * MOE_ROWS
        pend = jnp.cumsum(pcounts)
        pstart = pend - pcounts
        eid = route[:, 0:2].astype(jnp.int32)
        start_of = jnp.sum(jnp.where(eid[..., None] == jnp.arange(N_EXPERTS), pstart, 0), axis=-1)
        dest = (start_of + route[:, 2:4].astype(jnp.int32)).reshape(-1)
        starts = jnp.arange(nb, dtype=jnp.int32) * MOE_ROWS
        bexp = jnp.minimum(jnp.sum(pend[None, :] <= starts[:, None], axis=1), N_EXPERTS - 1).astype(jnp.int32)
        nused = (pend[-1:] // MOE_ROWS).astype(jnp.int32)

        xb = _dispatch(dest, h2, xb_zero)
        yb = _experts(bexp, nused, xb, exp_w_gate, exp_w_up, exp_w_down, l)
        xf = _combine(dest, x_mid, route, g2, fnw, yb, S, final=(l == DEPTH - 1))
    return xf.reshape(B, S, D)
```

```python
import functools
import math

import numpy as np
import jax
import jax.numpy as jnp
from jax import lax
from jax.experimental import pallas as pl
from jax.experimental.pallas import tpu as pltpu

F32 = jnp.float32
BF16 = jnp.bfloat16
HIGHEST = lax.Precision.HIGHEST

D_MODEL = 1024
DEPTH = 2
MLA_HEADS = 4
MLA_Q_LORA = 256
MLA_KV_LORA = 128
MLA_D_NOPE = 128
MLA_D_ROPE = 64
MLA_D_V = 128
ROPE_THETA = 10000.0
DIFF_HEADS = 4
DIFF_D_QK = 64
DIFF_D_V = 128
SSD_HEADS = 16
SSD_HEAD_DIM = 64
SSD_D_INNER = 1024
SSD_GROUPS = 4
SSD_STATE = 128
SSD_CONV = 4
SSD_CHUNK = 128
SSD_CONV_DIM = SSD_D_INNER + 2 * SSD_GROUPS * SSD_STATE
MOE_GROUPS = 4
MOE_EXPERTS_PER_GROUP = 8
N_EXPERTS = 32
D_EXPERT = 512
NORM_EPS = 1e-6
SUBLN_EPS = 1e-5

LANES = 128
SUBLANES = 8
MOE_ROWS = 256
FLASH_TQ = 1024
FLASH_TQ_DIFF = 512
LOG2E = math.log2(math.e)
VMEM_LIMIT = 48 * 1024 * 1024

_OFF = np.cumsum([0, MLA_Q_LORA, MLA_KV_LORA, MLA_D_ROPE, 512, 512, 512, SSD_D_INNER, SSD_CONV_DIM, SSD_HEADS])


def _cparams(sem):
    return pltpu.CompilerParams(dimension_semantics=sem, vmem_limit_bytes=VMEM_LIMIT)


def _dot(a, b, **kw):
    return jnp.dot(a, b, preferred_element_type=F32, **kw)


def _dot_nt(a, b):
    return lax.dot_general(a, b, (((1,), (1,)), ((), ())), preferred_element_type=F32)


def _rms(x, eps):
    return x * lax.rsqrt(jnp.mean(x * x, axis=-1, keepdims=True) + eps)


def _sigmoid(x):
    return 1.0 / (1.0 + jnp.exp(-x))


def _split_bf16(x, n):
    pieces = []
    for _ in range(n):
        head = x.astype(BF16)
        pieces.append(head)
        x = x - head.astype(F32)
    return pieces


def _ada_kernel(c_ref, w_ref, b_ref, o_ref):
    c = c_ref[...]
    cond = c * _sigmoid(c)
    o_ref[...] = _dot(cond, w_ref[...], precision=HIGHEST) + b_ref[...]


def _ada(c_pad, w_ada, b_ada):
    L, D, N = w_ada.shape
    tn = 1024
    return pl.pallas_call(
        _ada_kernel,
        grid=(L, N // tn),
        in_specs=[
            pl.BlockSpec((SUBLANES, D), lambda l, j: (0, 0)),
            pl.BlockSpec((None, D, tn), lambda l, j: (l, 0, j)),
            pl.BlockSpec((None, 1, tn), lambda l, j: (l, 0, j)),
        ],
        out_specs=pl.BlockSpec((None, SUBLANES, tn), lambda l, j: (l, 0, j)),
        out_shape=jax.ShapeDtypeStruct((L, SUBLANES, N), F32),
        compiler_params=_cparams(("parallel", "parallel")),
        name="ada",
    )(c_pad, w_ada, b_ada.reshape(L, 1, N))


def _inproj_kernel(x_ref, nw_ref, sh_ref, sc_ref, wm_ref, wq_ref, wk_ref, wv_ref, wz_ref, wx_ref, wt_ref,
                   om_ref, oq_ref, ok_ref, ov_ref, oz_ref, ox_ref, ot_ref):
    x = x_ref[...]
    h = _rms(x, NORM_EPS) * nw_ref[...] * (1.0 + sc_ref[...]) + sh_ref[...]
    hb = h.astype(BF16)
    for w_ref, o_ref in ((wm_ref, om_ref), (wq_ref, oq_ref), (wk_ref, ok_ref), (wv_ref, ov_ref),
                         (wz_ref, oz_ref), (wx_ref, ox_ref), (wt_ref, ot_ref)):
        o_ref[...] = _dot(hb, w_ref[...]).astype(o_ref.dtype)


def _inproj(xf, nw, sh, sc, ws, seq):
    T, D = xf.shape
    tm = 256
    per_b = seq // tm
    widths = [w.shape[1] for w in ws]
    row = lambda i: (i, 0)
    const = lambda i: (0, 0)
    bat = lambda i: (i // per_b, 0, 0)
    in_specs = [pl.BlockSpec((tm, D), row), pl.BlockSpec((1, D), const),
                pl.BlockSpec((None, 1, D), bat), pl.BlockSpec((None, 1, D), bat)]
    in_specs += [pl.BlockSpec((D, n), const) for n in widths]
    out_dt = [BF16] * 6 + [F32]
    return pl.pallas_call(
        _inproj_kernel,
        grid=(T // tm,),
        in_specs=in_specs,
        out_specs=[pl.BlockSpec((tm, n), row) for n in widths],
        out_shape=[jax.ShapeDtypeStruct((T, n), dt) for n, dt in zip(widths, out_dt)],
        compiler_params=_cparams(("parallel",)),
        name="inproj",
    )(xf, nw, sh, sc, *ws)


def _mla_prep_kernel(a_ref, pos_ref, inv_ref, qnw_ref, kvnw_ref, wq_ref, wkv_ref, q_ref, k_ref, v_ref):
    a = a_ref[...].astype(F32)
    cqn = (_rms(a[:, :256], NORM_EPS) * qnw_ref[...]).astype(BF16)
    ckvn = (_rms(a[:, 256:384], NORM_EPS) * kvnw_ref[...]).astype(BF16)
    ang = pos_ref[...] * inv_ref[...]
    lane = lax.broadcasted_iota(jnp.int32, ang.shape, 1)
    cs = jnp.where(lane < MLA_D_ROPE, jnp.cos(ang), jnp.sin(ang))
    kp = a[:, 384:512] * cs
    krkr = kp + pltpu.roll(kp, MLA_D_ROPE, axis=1)
    scale = (MLA_D_NOPE + MLA_D_ROPE) ** -0.5 * LOG2E
    qall = _dot(cqn, wq_ref[...]) * scale
    kvall = _dot(ckvn, wkv_ref[...])
    for h in range(MLA_HEADS):
        o = h * 256
        q_ref[h] = jnp.concatenate([qall[:, o:o + 128], qall[:, o + 128:o + 256] * cs], axis=1).astype(BF16)
        k_ref[h] = jnp.concatenate([kvall[:, o:o + 128], krkr], axis=1).astype(BF16)
        v_ref[h] = kvall[:, o + 128:o + 256].astype(BF16)


def _mla_prep(a, pos_col, inv, qnw, kvnw, wq, wkv):
    T = a.shape[0]
    tm = 512
    row = lambda i: (i, 0)
    const = lambda i: (0, 0)
    hrow = lambda i: (0, i, 0)
    return pl.pallas_call(
        _mla_prep_kernel,
        grid=(T // tm,),
        in_specs=[pl.BlockSpec((tm, 512), row), pl.BlockSpec((tm, 1), row), pl.BlockSpec((1, LANES), const),
                  pl.BlockSpec((1, 256), const), pl.BlockSpec((1, 128), const),
                  pl.BlockSpec((256, 1024), const), pl.BlockSpec((128, 1024), const)],
        out_specs=[pl.BlockSpec((MLA_HEADS, tm, 256), hrow), pl.BlockSpec((MLA_HEADS, tm, 256), hrow),
                   pl.BlockSpec((MLA_HEADS, tm, 128), hrow)],
        out_shape=[jax.ShapeDtypeStruct((MLA_HEADS, T, 256), BF16), jax.ShapeDtypeStruct((MLA_HEADS, T, 256), BF16),
                   jax.ShapeDtypeStruct((MLA_HEADS, T, 128), BF16)],
        compiler_params=_cparams(("parallel",)),
        name="mla_prep",
    )(a, pos_col, inv, qnw, kvnw, wq, wkv)


FLASH_STRIP = LANES


def _flash_init(m_s, l_s, acc_s, p_s, a_s, s_s):
    del s_s
    m_s[...] = jnp.full(m_s.shape, -jnp.inf, F32)
    l_s[...] = jnp.zeros(l_s.shape, F32)
    acc_s[...] = jnp.zeros(acc_s.shape, F32)
    p_s[1] = jnp.zeros(p_s.shape[1:], BF16)
    a_s[1] = jnp.ones(a_s.shape[1:], F32)


def _flash_pv(v_t, slot, acc_s, p_s, a_s):
    acc_s[...] = a_s[slot] * acc_s[...] + _dot(v_t, p_s[slot])


def _flash_stage(next_scores, strip_fn, v_prev, cur, m_s, l_s, acc_s, p_s, a_s, s_s):
    if next_scores is not None:
        s_s[1 - cur] = next_scores()
    _flash_pv(v_prev, 1 - cur, acc_s, p_s, a_s)
    for j in range(m_s.shape[1] // FLASH_STRIP):
        sl = slice(j * FLASH_STRIP, (j + 1) * FLASH_STRIP)
        s, bias, visible = strip_fn(s_s, cur, j)
        keep = (lambda x: x) if visible is None else (lambda x: jnp.where(visible, x, -jnp.inf))
        m_old = m_s[:, sl]
        t = s if bias is None else s - bias
        m_new = jnp.maximum(m_old, jnp.max(keep(t), axis=0, keepdims=True))
        alpha = jnp.exp2(m_old - m_new)
        p = jnp.exp2(keep(t - m_new))
        l_s[:, sl] = alpha * l_s[:, sl] + jnp.sum(p, axis=0, keepdims=True)
        m_s[:, sl] = m_new
        p_s[cur, :, sl] = p.astype(BF16)
        a_s[cur, :, sl] = alpha


def _flash_run(qi, scores, stage, flush, s_s):
    if scores is not None:
        s_s[0] = scores(0)

    def pair(i, _):
        a = 2 * i
        stage(a + 1, a, False, 0)
        stage(a + 2, a + 1, False, 1)
        return 0

    lax.fori_loop(0, qi // 2, pair, 0)

    @pl.when(qi % 2 == 0)
    def _():
        stage(None, qi, True, 0)
        flush(qi, 0)

    @pl.when(qi % 2 == 1)
    def _():
        stage(qi, qi - 1, False, 0)
        stage(None, qi, True, 1)
        flush(qi, 1)


def _causal_strip(shape, j):
    r = lax.broadcasted_iota(jnp.int32, shape, 0)
    c = lax.broadcasted_iota(jnp.int32, shape, 1) + j * FLASH_STRIP
    return r <= c


def _flash_scratch(dv, tk, nq, score_slots=True):
    s_shape = (2, tk, nq) if score_slots else (2, SUBLANES, LANES)
    return [pltpu.VMEM((1, nq), F32), pltpu.VMEM((1, nq), F32), pltpu.VMEM((dv, nq), F32),
            pltpu.VMEM((2, tk, nq), BF16), pltpu.VMEM((2, 1, nq), F32), pltpu.VMEM(s_shape, F32)]


def _mla_flash_kernel(qt_ref, k_ref, vt_ref, o_ref, *state, tq):
    m_s, l_s, acc_s, p_s, a_s, s_s = state
    qi = pl.program_id(2)
    _flash_init(*state)
    q_t = qt_ref[...]

    def scores(kb):
        return _dot(k_ref[pl.ds(pl.multiple_of(kb * tq, tq), tq), :], q_t)

    def stage(nxt, kb, masked, cur):
        def strip(s_ref, slot, j):
            t = s_ref[slot, :, j * FLASH_STRIP:(j + 1) * FLASH_STRIP]
            return t, None, (_causal_strip(t.shape, j) if masked else None)

        _flash_stage(None if nxt is None else (lambda: scores(nxt)), strip, vt_ref[jnp.maximum(kb - 1, 0)], cur,
                     *state)

    _flash_run(qi, scores, stage, lambda kb, slot: _flash_pv(vt_ref[kb], slot, acc_s, p_s, a_s), s_s)
    o_ref[...] = (acc_s[...] * (1.0 / l_s[...])).T.astype(o_ref.dtype)


def _mla_flash(q_t, k, v_t, batch, seq, tq):
    nq = seq // tq
    return pl.pallas_call(
        functools.partial(_mla_flash_kernel, tq=tq),
        grid=(batch, MLA_HEADS, nq),
        in_specs=[pl.BlockSpec((None, 256, tq), lambda b, h, i: (h, 0, b * nq + i)),
                  pl.BlockSpec((None, seq, 256), lambda b, h, i: (h, b, 0)),
                  pl.BlockSpec((None, nq, MLA_D_V, tq), lambda b, h, i: (h, b, 0, 0))],
        out_specs=pl.BlockSpec((tq, MLA_D_V), lambda b, h, i: (b * nq + i, h)),
        out_shape=jax.ShapeDtypeStruct((batch * seq, MLA_HEADS * MLA_D_V), BF16),
        scratch_shapes=_flash_scratch(MLA_D_V, tq, tq),
        compiler_params=_cparams(("parallel", "parallel", "arbitrary")),
        name="mla_flash",
    )(q_t, k, v_t)


def _diff_flash_kernel(qt_ref, k_ref, vt_ref, pq_ref, pk_ref, slope_ref, lam_ref, sw_ref, o_ref,
                       m_s, l_s, acc_s, p_s, a_s, s_s, kp_s, *, tq, lam_init, ordered):
    state = (m_s, l_s, acc_s, p_s, a_s, s_s)
    qi = pl.program_id(2)
    _flash_init(*state)
    q_t = qt_ref[...]
    row = lax.broadcasted_iota(jnp.int32, q_t.shape, 0)
    zero = jnp.zeros_like(q_t)
    q2_t = jnp.concatenate([jnp.where(row < DIFF_D_QK, q_t, zero), jnp.where(row >= DIFF_D_QK, q_t, zero)], axis=1)
    spm = tq // FLASH_STRIP
    if ordered:
        ones = jnp.where(lax.broadcasted_iota(jnp.int32, q2_t.shape, 0) < 3, 1.0, 0.0).astype(BF16)
        q2_t = jnp.concatenate([q2_t, ones], axis=0)

        @pl.when(qi == 0)
        def _():
            lane = lax.broadcasted_iota(jnp.int32, (tq, LANES), 1)

            def fill(kb, _):
                pieces = _split_bf16(slope_ref[...] * pk_ref[kb], 3)
                tile = jnp.zeros((tq, LANES), F32)
                for i, piece in enumerate(pieces):
                    tile = jnp.where(lane == i, piece.astype(F32), tile)
                kp_s[pl.ds(pl.multiple_of(kb * tq, tq), tq), :] = tile.astype(BF16)
                return 0

            lax.fori_loop(0, pk_ref.shape[0], fill, 0)
    else:
        slope = slope_ref[...]
        spq = slope * pq_ref[...]

    def scores(kb):
        rows = pl.ds(pl.multiple_of(kb * tq, tq), tq)
        k = k_ref[rows, :]
        if ordered:
            k = jnp.concatenate([k, kp_s[rows, :]], axis=1)
        return _dot(k, q2_t)

    def stage(nxt, kb, masked, cur):
        del nxt
        s_t = scores(kb)
        if not ordered:
            bias = jnp.abs(slope * pk_ref[kb] - spq)

        def strip(s_ref, slot, j):
            jq = j % spm
            t = s_t[:, j * FLASH_STRIP:(j + 1) * FLASH_STRIP]
            return (t, None if ordered else bias[:, jq * FLASH_STRIP:(jq + 1) * FLASH_STRIP],
                    _causal_strip(t.shape, jq) if masked else None)

        _flash_stage(None, strip, vt_ref[jnp.maximum(kb - 1, 0)], cur, *state)

    _flash_run(qi, None, stage, lambda kb, slot: _flash_pv(vt_ref[kb], slot, acc_s, p_s, a_s), s_s)
    o2 = acc_s[...] * (1.0 / l_s[...])
    lf = lam_ref[...]
    lam = (jnp.exp(jnp.sum(lf[0:1] * lf[1:2], keepdims=True))
           - jnp.exp(jnp.sum(lf[2:3] * lf[3:4], keepdims=True)) + lam_init)
    out = o2[:, :tq] - lam * o2[:, tq:]
    out = out * lax.rsqrt(jnp.mean(out * out, axis=0, keepdims=True) + SUBLN_EPS) * sw_ref[...] * (1.0 - lam_init)
    o_ref[...] = out.T.astype(o_ref.dtype)


def _diff_flash(q_t, k, v_t, pos_row, pos_kcol, slopes, lam, sw_col, batch, seq, tq, lam_init, ordered):
    nq = seq // tq
    return pl.pallas_call(
        functools.partial(_diff_flash_kernel, tq=tq, lam_init=lam_init, ordered=ordered),
        grid=(batch, DIFF_HEADS, nq),
        in_specs=[pl.BlockSpec((128, tq), lambda b, h, i: (h, b * nq + i)),
                  pl.BlockSpec((seq, 128), lambda b, h, i: (b, h)),
                  pl.BlockSpec((nq, DIFF_D_V, tq), lambda b, h, i: (b, h, 0)),
                  pl.BlockSpec((None, 1, tq), lambda b, h, i: (b, 0, i)),
                  pl.BlockSpec((nq, tq, 1), lambda b, h, i: (b, 0, 0)),
                  pl.BlockSpec((None, 1, 1), lambda b, h, i: (h, 0, 0)),
                  pl.BlockSpec((4, DIFF_D_QK), lambda b, h, i: (0, 0)),
                  pl.BlockSpec((DIFF_D_V, 1), lambda b, h, i: (0, 0))],
        out_specs=pl.BlockSpec((tq, DIFF_D_V), lambda b, h, i: (b * nq + i, h)),
        out_shape=jax.ShapeDtypeStruct((batch * seq, DIFF_HEADS * DIFF_D_V), BF16),
        scratch_shapes=_flash_scratch(DIFF_D_V, tq, 2 * tq, score_slots=False)
        + [pltpu.VMEM((seq, LANES) if ordered else (SUBLANES * 2, LANES), BF16)],
        compiler_params=_cparams(("parallel", "parallel", "arbitrary")),
        name="diff_flash",
    )(q_t, k, v_t, pos_row, pos_kcol, slopes, lam, sw_col)


def _ssd_kernel(xbc_ref, z_ref, dt_ref, cw_ref, cb_ref, dtb_ref, alog_ref, dsk_ref, nw_ref, e_ref, o_ref,
                ext, state):
    L = SSD_CHUNK
    G, N = SSD_GROUPS, SSD_STATE
    GW = SSD_D_INNER // G

    @pl.when(pl.program_id(1) == 0)
    def _():
        ext[0:SUBLANES, :] = jnp.zeros((SUBLANES, SSD_CONV_DIM), F32)
        state[...] = jnp.zeros(state.shape, F32)

    u = xbc_ref[...].astype(F32)
    ext[SUBLANES:SUBLANES + L, :] = u
    acc = cb_ref[...] + cw_ref[SSD_CONV - 1:SSD_CONV, :] * u
    for k in range(SSD_CONV - 1):
        o = SUBLANES - (SSD_CONV - 1) + k
        acc = acc + cw_ref[k:k + 1, :] * ext[o:o + L, :]
    ext[0:SUBLANES, :] = u[L - SUBLANES:L, :]
    act = acc * _sigmoid(acc)
    xs = act[:, :SSD_D_INNER]
    bm = act[:, SSD_D_INNER:SSD_D_INNER + G * N]
    cm = act[:, SSD_D_INNER + G * N:]

    dtr = dt_ref[...] + dtb_ref[...]
    dtv = jnp.maximum(dtr, 0.0) + jnp.log1p(jnp.exp(-jnp.abs(dtr)))
    adt = dtv * (-jnp.exp(alog_ref[...]))
    r = lax.broadcasted_iota(jnp.int32, (L, L), 0)
    c = lax.broadcasted_iota(jnp.int32, (L, L), 1)
    tril = r >= c
    a_cs = _dot(tril.astype(F32), adt, precision=HIGHEST)
    a_cs_t = a_cs.T
    ea = jnp.exp(a_cs)
    dte = jnp.exp(a_cs[L - 1:L, :] - a_cs)
    stack = jnp.concatenate([dtv, dte, ea], axis=0)
    hi, lo = _split_bf16(stack, 2)
    ex = _dot(hi, e_ref[...]) + _dot(lo, e_ref[...])
    dt_e, dte_e, ea_e = ex[:L], ex[L:2 * L], ex[2 * L:]
    xdt = xs * dt_e
    xdt_b = xdt.astype(BF16)
    xw_b = (xdt * dte_e).astype(BF16)
    bb = bm.astype(BF16)
    cbf = cm.astype(BF16)
    lane_g = lax.broadcasted_iota(jnp.int32, (1, GW), 1)
    ys = []
    for g in range(G):
        bg = bb[:, g * N:(g + 1) * N]
        cg = cbf[:, g * N:(g + 1) * N]
        cb = _dot_nt(cg, bg)
        xg = xdt_b[:, g * GW:(g + 1) * GW]
        yd = jnp.zeros((L, GW), F32)
        for rr in range(SSD_HEADS // G):
            h = g * (SSD_HEADS // G) + rr
            seg = a_cs[:, h:h + 1] - a_cs_t[h:h + 1, :]
            dec = jnp.exp(jnp.where(tril, seg, -jnp.inf))
            in_head = (lane_g >= rr * SSD_HEAD_DIM) & (lane_g < (rr + 1) * SSD_HEAD_DIM)
            yd = yd + _dot((cb * dec).astype(BF16), jnp.where(in_head, xg, jnp.zeros_like(xg)))
        st_old = state[g]
        eg = ea_e[:, g * GW:(g + 1) * GW]
        yoff = _dot(cg, st_old.astype(BF16)) * eg
        bg_t = bm[:, g * N:(g + 1) * N].T.astype(BF16)
        state[g] = st_old * eg[L - 1:L, :] + _dot(bg_t, xw_b[:, g * GW:(g + 1) * GW])
        ys.append(yd + yoff)
    y = jnp.concatenate(ys, axis=1) + xs * dsk_ref[...]
    zf = z_ref[...].astype(F32)
    y = y * (zf * _sigmoid(zf))
    y = jnp.concatenate([_rms(y[:, g * GW:(g + 1) * GW], NORM_EPS) for g in range(G)], axis=1)
    o_ref[...] = (y * nw_ref[...]).astype(o_ref.dtype)


def _ssd(xbc, z, dt, cw, cb, dtb, alog, dsk, nw, emat, batch, seq):
    L = SSD_CHUNK
    nc = seq // L
    row = lambda b, c: (b * nc + c, 0)
    const = lambda b, c: (0, 0)
    return pl.pallas_call(
        _ssd_kernel,
        grid=(batch, nc),
        in_specs=[pl.BlockSpec((L, SSD_CONV_DIM), row), pl.BlockSpec((L, SSD_D_INNER), row),
                  pl.BlockSpec((L, LANES), row),
                  pl.BlockSpec((SSD_CONV, SSD_CONV_DIM), const), pl.BlockSpec((1, SSD_CONV_DIM), const),
                  pl.BlockSpec((1, LANES), const), pl.BlockSpec((1, LANES), const),
                  pl.BlockSpec((1, SSD_D_INNER), const), pl.BlockSpec((1, SSD_D_INNER), const),
                  pl.BlockSpec((LANES, SSD_D_INNER), const)],
        out_specs=pl.BlockSpec((L, SSD_D_INNER), row),
        out_shape=jax.ShapeDtypeStruct((batch * seq, SSD_D_INNER), BF16),
        scratch_shapes=[pltpu.VMEM((SUBLANES + L, SSD_CONV_DIM), F32),
                        pltpu.VMEM((SSD_GROUPS, SSD_STATE, SSD_D_INNER // SSD_GROUPS), F32)],
        compiler_params=_cparams(("arbitrary", "arbitrary")),
        name="ssd",
    )(xbc, z, dt, cw, cb, dtb, alog, dsk, nw, emat)


def _outproj_router_kernel(x_ref, ym_ref, yd_ref, ys_ref, g1_ref, sh_ref, sc_ref, nw_ref, wm_ref, wd_ref, ws_ref,
                           wr_ref, br_ref, xo_ref, h_ref, route_ref, cnt_ref, carry, wr_s):
    @pl.when(pl.program_id(0) == 0)
    def _():
        carry[...] = jnp.zeros(carry.shape, F32)
        wr_s[0], wr_s[1] = _split_bf16(wr_ref[...], 2)

    y = _dot(ym_ref[...], wm_ref[...]) + _dot(yd_ref[...], wd_ref[...]) + _dot(ys_ref[...], ws_ref[...])
    xn = x_ref[...] + g1_ref[...] * y
    xo_ref[...] = xn
    h = _rms(xn, NORM_EPS) * nw_ref[...] * (1.0 + sc_ref[...]) + sh_ref[...]
    h_ref[...] = h
    h_hi, h_lo = _split_bf16(h, 2)
    logits = _dot(h_hi, wr_s[0]) + (_dot(h_lo, wr_s[0]) + _dot(h_hi, wr_s[1])) + br_ref[...]
    tm = logits.shape[0]
    lane = lax.broadcasted_iota(jnp.int32, logits.shape, 1)
    big = jnp.int32(1 << 20)
    neg = -jnp.inf
    lg = jnp.where((lane >= N_EXPERTS) & (lane < N_EXPERTS + MOE_GROUPS), logits, neg)
    mg = jnp.max(lg, axis=-1, keepdims=True)
    pg_top = 1.0 / jnp.sum(jnp.exp(lg - mg), axis=-1, keepdims=True)
    gsel = jnp.min(jnp.where(lg == mg, lane, big), axis=-1, keepdims=True) - N_EXPERTS
    le = jnp.where((lane < N_EXPERTS) & ((lane // MOE_EXPERTS_PER_GROUP) == gsel), logits, neg)
    m1 = jnp.max(le, axis=-1, keepdims=True)
    e1 = jnp.min(jnp.where(le == m1, lane, big), axis=-1, keepdims=True)
    le2 = jnp.where(lane == e1, neg, le)
    m2 = jnp.max(le2, axis=-1, keepdims=True)
    e2 = jnp.min(jnp.where(le2 == m2, lane, big), axis=-1, keepdims=True)
    rr = jnp.exp(m2 - m1)
    gate1 = pg_top / (1.0 + rr)
    gate2 = pg_top * rr / (1.0 + rr)
    onehot = jnp.where((lane == e1) | (lane == e2), 1.0, 0.0)
    ri = lax.broadcasted_iota(jnp.int32, (tm, tm), 0)
    ci = lax.broadcasted_iota(jnp.int32, (tm, tm), 1)
    before = jnp.where(ri > ci, 1.0, 0.0).astype(BF16)
    prefix = _dot(before, onehot.astype(BF16)) + carry[...]
    rank1 = jnp.sum(jnp.where(lane == e1, prefix, 0.0), axis=-1, keepdims=True)
    rank2 = jnp.sum(jnp.where(lane == e2, prefix, 0.0), axis=-1, keepdims=True)
    carry[...] = carry[...] + jnp.sum(onehot, axis=0, keepdims=True)
    cnt_ref[...] = carry[...]
    cols = (e1.astype(F32), e2.astype(F32), rank1, rank2, gate1, gate2)
    route = jnp.zeros(logits.shape, F32)
    for i, col in enumerate(cols):
        route = jnp.where(lane == i, col, route)
    route_ref[...] = route


def _outproj_router(xf, ym, yd, ys, g1, sh2, sc2, n2w, wom, wod, wos, wr, br, seq):
    T, D = xf.shape
    tm = 256
    per_b = seq // tm
    row = lambda i: (i, 0)
    const = lambda i: (0, 0)
    bat = lambda i: (i // per_b, 0, 0)
    return pl.pallas_call(
        _outproj_router_kernel,
        grid=(T // tm,),
        in_specs=[pl.BlockSpec((tm, D), row), pl.BlockSpec((tm, 512), row), pl.BlockSpec((tm, 512), row),
                  pl.BlockSpec((tm, SSD_D_INNER), row),
                  pl.BlockSpec((None, 1, D), bat), pl.BlockSpec((None, 1, D), bat), pl.BlockSpec((None, 1, D), bat),
                  pl.BlockSpec((1, D), const),
                  pl.BlockSpec((512, D), const), pl.BlockSpec((512, D), const), pl.BlockSpec((SSD_D_INNER, D), const),
                  pl.BlockSpec((D, LANES), const), pl.BlockSpec((1, LANES), const)],
        out_specs=[pl.BlockSpec((tm, D), row), pl.BlockSpec((tm, D), row), pl.BlockSpec((tm, LANES), row),
                   pl.BlockSpec((1, LANES), const)],
        out_shape=[jax.ShapeDtypeStruct((T, D), F32), jax.ShapeDtypeStruct((T, D), F32),
                   jax.ShapeDtypeStruct((T, LANES), F32), jax.ShapeDtypeStruct((1, LANES), F32)],
        scratch_shapes=[pltpu.VMEM((1, LANES), F32), pltpu.VMEM((2, D, LANES), BF16)],
        compiler_params=_cparams(("arbitrary",)),
        name="outproj_router",
    )(xf, ym, yd, ys, g1, sh2, sc2, n2w, wom, wod, wos, wr, br)


def _row_copy(src, dst, i_src, i_dst, sem):
    return pltpu.make_async_copy(src.at[pl.ds(i_src, 1)], dst.at[pl.ds(i_dst, 1)], sem)


def _dispatch_kernel(dest_ref, h_ref, xb_in_ref, xb_ref, sem):
    del xb_in_ref
    tm = h_ref.shape[0]

    def issue(r, _):
        for k in range(2):
            _row_copy(h_ref, xb_ref, r, dest_ref[2 * r + k], sem).start()
        return 0

    lax.fori_loop(0, tm, issue, 0, unroll=8)

    def drain(r, _):
        for k in range(2):
            _row_copy(h_ref, xb_ref, 0, 0, sem).wait()
        return 0

    lax.fori_loop(0, tm, drain, 0, unroll=8)


def _dispatch(dest, h, xb_zero):
    T, D = h.shape
    tm = 256
    return pl.pallas_call(
        _dispatch_kernel,
        grid=(T // tm,),
        in_specs=[pl.BlockSpec((2 * tm,), lambda i: (i,), memory_space=pltpu.SMEM),
                  pl.BlockSpec((tm, D), lambda i: (i, 0)),
                  pl.BlockSpec(memory_space=pl.ANY)],
        out_specs=pl.BlockSpec(memory_space=pl.ANY),
        out_shape=jax.ShapeDtypeStruct(xb_zero.shape, F32),
        scratch_shapes=[pltpu.SemaphoreType.DMA(())],
        input_output_aliases={2: 0},
        compiler_params=_cparams(("arbitrary",)),
        name="moe_dispatch",
    )(dest, h, xb_zero)


def _expert_kernel(bexp_ref, nused_ref, x_ref, wg_ref, wu_ref, wd_ref, o_ref, wg_s, wu_s, wd_s):
    j = pl.program_id(0)
    used = j < nused_ref[0]
    new_expert = (j == 0) | (bexp_ref[j] != bexp_ref[jnp.maximum(j - 1, 0)])

    @pl.when(used & new_expert)
    def _():
        wg_s[...] = wg_ref[...].astype(BF16)
        wu_s[...] = wu_ref[...].astype(BF16)
        wd_s[...] = wd_ref[...].astype(BF16)

    @pl.when(used)
    def _():
        x = x_ref[...].astype(BF16)
        g = _dot(x, wg_s[...])
        u = _dot(x, wu_s[...])
        o_ref[...] = _dot((g * _sigmoid(g) * u).astype(BF16), wd_s[...])

    @pl.when(jnp.logical_not(used))
    def _():
        o_ref[...] = jnp.zeros(o_ref.shape, F32)


def _experts(bexp, nused, xb, w_gate, w_up, w_down, layer):
    cap, D = xb.shape
    nb = cap // MOE_ROWS
    F = D_EXPERT
    last = lambda j, be, nu: jnp.maximum(jnp.minimum(j, nu[0] - 1), 0)
    return pl.pallas_call(
        _expert_kernel,
        grid_spec=pltpu.PrefetchScalarGridSpec(
            num_scalar_prefetch=2,
            grid=(nb,),
            in_specs=[pl.BlockSpec((MOE_ROWS, D), lambda j, be, nu: (last(j, be, nu), 0)),
                      pl.BlockSpec((None, None, D, F), lambda j, be, nu: (layer, be[last(j, be, nu)], 0, 0)),
                      pl.BlockSpec((None, None, D, F), lambda j, be, nu: (layer, be[last(j, be, nu)], 0, 0)),
                      pl.BlockSpec((None, None, F, D), lambda j, be, nu: (layer, be[last(j, be, nu)], 0, 0))],
            out_specs=pl.BlockSpec((MOE_ROWS, D), lambda j, be, nu: (j, 0)),
            scratch_shapes=[pltpu.VMEM((D, F), BF16), pltpu.VMEM((D, F), BF16), pltpu.VMEM((F, D), BF16)],
        ),
        out_shape=jax.ShapeDtypeStruct((cap, D), F32),
        compiler_params=_cparams(("arbitrary",)),
        name="moe_experts",
    )(bexp, nused, xb, w_gate, w_up, w_down)


def _combine_kernel(dest_ref, x_ref, route_ref, g2_ref, fnw_ref, yb_ref, o_ref, gbuf, sem, *, final):
    tm = x_ref.shape[0]

    def issue(r, _):
        for k in range(2):
            _row_copy(yb_ref, gbuf.at[k], dest_ref[2 * r + k], r, sem).start()
        return 0

    lax.fori_loop(0, tm, issue, 0, unroll=8)

    def drain(r, _):
        for k in range(2):
            _row_copy(yb_ref, gbuf.at[k], 0, 0, sem).wait()
        return 0

    lax.fori_loop(0, tm, drain, 0, unroll=8)
    route = route_ref[...]
    y = route[:, 4:5] * gbuf[0] + route[:, 5:6] * gbuf[1]
    xo = x_ref[...] + g2_ref[...] * y
    if final:
        xo = _rms(xo, NORM_EPS) * fnw_ref[...]
    o_ref[...] = xo


def _combine(dest, xf, route, g2, fnw, yb, seq, final):
    T, D = xf.shape
    tm = 256
    per_b = seq // tm
    return pl.pallas_call(
        functools.partial(_combine_kernel, final=final),
        grid=(T // tm,),
        in_specs=[pl.BlockSpec((2 * tm,), lambda i: (i,), memory_space=pltpu.SMEM),
                  pl.BlockSpec((tm, D), lambda i: (i, 0)),
                  pl.BlockSpec((tm, LANES), lambda i: (i, 0)),
                  pl.BlockSpec((None, 1, D), lambda i: (i // per_b, 0, 0)),
                  pl.BlockSpec((1, D), lambda i: (0, 0)),
                  pl.BlockSpec(memory_space=pl.ANY)],
        out_specs=pl.BlockSpec((tm, D), lambda i: (i, 0)),
        out_shape=jax.ShapeDtypeStruct((T, D), F32),
        scratch_shapes=[pltpu.VMEM((2, tm, D), F32), pltpu.SemaphoreType.DMA(())],
        compiler_params=_cparams(("arbitrary",)),
        name="moe_combine",
    )(dest, xf, route, g2, fnw, yb)


def _rot_cols(w):
    half = w.shape[1] // 2
    return jnp.concatenate([-w[:, half:], w[:, :half]], axis=1)


def _pad_lanes(a, n=LANES):
    return jnp.pad(a, [(0, 0)] * (a.ndim - 1) + [(0, n - a.shape[-1])])


def kernel(x, c, positions, w_ada, b_ada, norm1_w, w_in, mla_q_norm_w, mla_w_uq, mla_kv_norm_w, mla_w_ukv, diff_lambda, diff_subln_w, ssd_conv_w, ssd_conv_b, ssd_dt_bias, ssd_a_log, ssd_d, ssd_norm_w, w_out, norm2_w, router_w_group, router_b_group, router_w_expert, router_b_expert, exp_w_gate, exp_w_up, exp_w_down, final_norm_w):
    B, S, D = x.shape
    T = B * S
    xf = x.reshape(T, D)
    mod = _ada(jnp.pad(c, ((0, SUBLANES - B), (0, 0))), w_ada, b_ada)[:, :B]

    pos_f = positions.astype(F32)
    pos_col = pos_f.reshape(T, 1)
    inv = 1.0 / (ROPE_THETA ** (jnp.arange(0, MLA_D_ROPE, 2, dtype=F32) / MLA_D_ROPE))
    inv = jnp.tile(inv, 4).reshape(1, LANES)
    slopes = jnp.exp2(-8.0 / DIFF_HEADS * jnp.arange(1, DIFF_HEADS + 1, dtype=F32)).reshape(DIFF_HEADS, 1, 1)
    slopes = slopes * LOG2E
    tq = min(FLASH_TQ, S)
    tqd = min(FLASH_TQ_DIFF, S)
    pos_ordered = jnp.all(positions[:, 1:] >= positions[:, :-1])
    head_of_col = np.arange(SSD_D_INNER) // SSD_HEAD_DIM
    emat = jnp.asarray(np.arange(LANES)[:, None] == head_of_col[None, :], BF16)

    cap = T * 2 + N_EXPERTS * MOE_ROWS
    nb = cap // MOE_ROWS
    xb_zero = jnp.zeros((cap, D), F32)
    fnw = final_norm_w.reshape(1, D)

    for l in range(DEPTH):
        sh1, sc1, g1, sh2, sc2, g2 = [mod[l, :, i * D:(i + 1) * D].reshape(B, 1, D) for i in range(6)]
        w = w_in[l]
        ws = [jnp.concatenate([w[:, :_OFF[3]], _rot_cols(w[:, _OFF[2]:_OFF[3]])], axis=1)]
        ws.append(w[:, _OFF[3]:_OFF[4]] * (DIFF_D_QK ** -0.5 * LOG2E))
        ws += [w[:, _OFF[i]:_OFF[i + 1]] for i in range(4, 8)]
        ws.append(_pad_lanes(w[:, _OFF[8]:_OFF[9]]))
        ws = [a.astype(BF16) for a in ws]
        a_mla, dq, dk, dv, z, xbc, dt = _inproj(xf, norm1_w[l].reshape(1, D), sh1, sc1, ws, S)

        wq = mla_w_uq[l].reshape(MLA_Q_LORA, MLA_HEADS, MLA_D_NOPE + MLA_D_ROPE)
        wq_pe = wq[:, :, MLA_D_NOPE:]
        wq_rot = jnp.concatenate([-wq_pe[:, :, MLA_D_ROPE // 2:], wq_pe[:, :, :MLA_D_ROPE // 2]], axis=2)
        wq_cat = jnp.concatenate([wq, wq_rot], axis=2).reshape(MLA_Q_LORA, MLA_HEADS * 256).astype(BF16)
        q, k, v = _mla_prep(a_mla, pos_col, inv, mla_q_norm_w[l].reshape(1, -1), mla_kv_norm_w[l].reshape(1, -1),
                            wq_cat, mla_w_ukv[l].astype(BF16))
        q_t = q.transpose(0, 2, 1)
        v_t = v.reshape(MLA_HEADS, T // tq, tq, MLA_D_V).transpose(0, 1, 3, 2)
        y_mla = _mla_flash(q_t, k, v_t, B, S, tq)

        lam_init = 0.8 - 0.6 * math.exp(-0.3 * l)
        dv_t = dv.reshape(T // tqd, tqd, DIFF_HEADS * DIFF_D_V).transpose(0, 2, 1)
        diff_args = (dq.T, dk, dv_t, pos_f.reshape(B, 1, S), pos_f.reshape(T // tqd, tqd, 1), slopes,
                     diff_lambda[l], diff_subln_w[l].reshape(-1, 1))
        y_diff = lax.cond(pos_ordered,
                          lambda *a: _diff_flash(*a, B, S, tqd, lam_init, ordered=True),
                          lambda *a: _diff_flash(*a, B, S, tqd, lam_init, ordered=False), *diff_args)

        y_ssd = _ssd(xbc, z, dt, ssd_conv_w[l], ssd_conv_b[l].reshape(1, -1),
                     _pad_lanes(ssd_dt_bias[l].reshape(1, -1)), _pad_lanes(ssd_a_log[l].reshape(1, -1)),
                     jnp.repeat(ssd_d[l], SSD_HEAD_DIM).reshape(1, -1), ssd_norm_w[l].reshape(1, -1), emat, B, S)

        wo = w_out[l].astype(BF16)
        wr = _pad_lanes(jnp.concatenate([router_w_expert[l], router_w_group[l]], axis=1))
        br = _pad_lanes(jnp.concatenate([router_b_expert[l], router_b_group[l]]).reshape(1, -1))
        x_mid, h2, route, cnt = _outproj_router(xf, y_mla, y_diff, y_ssd, g1, sh2, sc2, norm2_w[l].reshape(1, D),
                                                wo[:512], wo[512:1024], wo[1024:], wr, br, S)

        counts = cnt[0, :N_EXPERTS].astype(jnp.int32)
        pcounts = (counts + MOE_ROWS - 1) // MOE_ROWS * MOE_ROWS
        pend = jnp.cumsum(pcounts)
        pstart = pend - pcounts
        eid = route[:, 0:2].astype(jnp.int32)
        start_of = jnp.sum(jnp.where(eid[..., None] == jnp.arange(N_EXPERTS), pstart, 0), axis=-1)
        dest = (start_of + route[:, 2:4].astype(jnp.int32)).reshape(-1)
        starts = jnp.arange(nb, dtype=jnp.int32) * MOE_ROWS
        bexp = jnp.minimum(jnp.sum(pend[None, :] <= starts[:, None], axis=1), N_EXPERTS - 1).astype(jnp.int32)
        nused = (pend[-1:] // MOE_ROWS).astype(jnp.int32)

        xb = _dispatch(dest, h2, xb_zero)
        yb = _experts(bexp, nused, xb, exp_w_gate, exp_w_up, exp_w_down, l)
        xf = _combine(dest, x_mid, route, g2, fnw, yb, S, final=(l == DEPTH - 1))
    return xf.reshape(B, S, D)
```

```python
import functools
import math

import numpy as np
import jax
import jax.numpy as jnp
from jax import lax
from jax.experimental import pallas as pl
from jax.experimental.pallas import tpu as pltpu

F32 = jnp.float32
BF16 = jnp.bfloat16
HIGHEST = lax.Precision.HIGHEST

D_MODEL = 1024
DEPTH = 2
MLA_HEADS = 4
MLA_Q_LORA = 256
MLA_KV_LORA = 128
MLA_D_NOPE = 128
MLA_D_ROPE = 64
MLA_D_V = 128
ROPE_THETA = 10000.0
DIFF_HEADS = 4
DIFF_D_QK = 64
DIFF_D_V = 128
SSD_HEADS = 16
SSD_HEAD_DIM = 64
SSD_D_INNER = 1024
SSD_GROUPS = 4
SSD_STATE = 128
SSD_CONV = 4
SSD_CHUNK = 128
SSD_CONV_DIM = SSD_D_INNER + 2 * SSD_GROUPS * SSD_STATE
MOE_GROUPS = 4
MOE_EXPERTS_PER_GROUP = 8
N_EXPERTS = 32
D_EXPERT = 512
NORM_EPS = 1e-6
SUBLN_EPS = 1e-5

LANES = 128
SUBLANES = 8
MOE_ROWS = 256
FLASH_TQ = 1024
FLASH_TQ_DIFF = 512
LOG2E = math.log2(math.e)
VMEM_LIMIT = 48 * 1024 * 1024

_OFF = np.cumsum([0, MLA_Q_LORA, MLA_KV_LORA, MLA_D_ROPE, 512, 512, 512, SSD_D_INNER, SSD_CONV_DIM, SSD_HEADS])


def _cparams(sem):
    return pltpu.CompilerParams(dimension_semantics=sem, vmem_limit_bytes=VMEM_LIMIT)


def _dot(a, b, **kw):
    return jnp.dot(a, b, preferred_element_type=F32, **kw)


def _dot_nt(a, b):
    return lax.dot_general(a, b, (((1,), (1,)), ((), ())), preferred_element_type=F32)


def _rms(x, eps):
    return x * lax.rsqrt(jnp.mean(x * x, axis=-1, keepdims=True) + eps)


def _sigmoid(x):
    return 1.0 / (1.0 + jnp.exp(-x))


def _split_bf16(x, n):
    pieces = []
    for _ in range(n):
        head = x.astype(BF16)
        pieces.append(head)
        x = x - head.astype(F32)
    return pieces


def _ada_kernel(c_ref, w_ref, b_ref, o_ref):
    c = c_ref[...]
    cond = c * _sigmoid(c)
    o_ref[...] = _dot(cond, w_ref[...], precision=HIGHEST) + b_ref[...]


def _ada(c_pad, w_ada, b_ada):
    L, D, N = w_ada.shape
    tn = 1024
    return pl.pallas_call(
        _ada_kernel,
        grid=(L, N // tn),
        in_specs=[
            pl.BlockSpec((SUBLANES, D), lambda l, j: (0, 0)),
            pl.BlockSpec((None, D, tn), lambda l, j: (l, 0, j)),
            pl.BlockSpec((None, 1, tn), lambda l, j: (l, 0, j)),
        ],
        out_specs=pl.BlockSpec((None, SUBLANES, tn), lambda l, j: (l, 0, j)),
        out_shape=jax.ShapeDtypeStruct((L, SUBLANES, N), F32),
        compiler_params=_cparams(("parallel", "parallel")),
        name="ada",
    )(c_pad, w_ada, b_ada.reshape(L, 1, N))


def _inproj_kernel(x_ref, nw_ref, sh_ref, sc_ref, wm_ref, wq_ref, wk_ref, wv_ref, wz_ref, wx_ref, wt_ref,
                   om_ref, oq_ref, ok_ref, ov_ref, oz_ref, ox_ref, ot_ref):
    x = x_ref[...]
    h = _rms(x, NORM_EPS) * nw_ref[...] * (1.0 + sc_ref[...]) + sh_ref[...]
    hb = h.astype(BF16)
    for w_ref, o_ref in ((wm_ref, om_ref), (wq_ref, oq_ref), (wk_ref, ok_ref), (wv_ref, ov_ref),
                         (wz_ref, oz_ref), (wx_ref, ox_ref), (wt_ref, ot_ref)):
        o_ref[...] = _dot(hb, w_ref[...]).astype(o_ref.dtype)


def _inproj(xf, nw, sh, sc, ws, seq):
    T, D = xf.shape
    tm = 256
    per_b = seq // tm
    widths = [w.shape[1] for w in ws]
    row = lambda i: (i, 0)
    const = lambda i: (0, 0)
    bat = lambda i: (i // per_b, 0, 0)
    in_specs = [pl.BlockSpec((tm, D), row), pl.BlockSpec((1, D), const),
                pl.BlockSpec((None, 1, D), bat), pl.BlockSpec((None, 1, D), bat)]
    in_specs += [pl.BlockSpec((D, n), const) for n in widths]
    out_dt = [BF16] * 6 + [F32]
    return pl.pallas_call(
        _inproj_kernel,
        grid=(T // tm,),
        in_specs=in_specs,
        out_specs=[pl.BlockSpec((tm, n), row) for n in widths],
        out_shape=[jax.ShapeDtypeStruct((T, n), dt) for n, dt in zip(widths, out_dt)],
        compiler_params=_cparams(("parallel",)),
        name="inproj",
    )(xf, nw, sh, sc, *ws)


def _mla_prep_kernel(a_ref, pos_ref, inv_ref, qnw_ref, kvnw_ref, wq_ref, wkv_ref, q_ref, k_ref, v_ref):
    a = a_ref[...].astype(F32)
    cqn = (_rms(a[:, :256], NORM_EPS) * qnw_ref[...]).astype(BF16)
    ckvn = (_rms(a[:, 256:384], NORM_EPS) * kvnw_ref[...]).astype(BF16)
    ang = pos_ref[...] * inv_ref[...]
    lane = lax.broadcasted_iota(jnp.int32, ang.shape, 1)
    cs = jnp.where(lane < MLA_D_ROPE, jnp.cos(ang), jnp.sin(ang))
    kp = a[:, 384:512] * cs
    krkr = kp + pltpu.roll(kp, MLA_D_ROPE, axis=1)
    scale = (MLA_D_NOPE + MLA_D_ROPE) ** -0.5 * LOG2E
    qall = _dot(cqn, wq_ref[...]) * scale
    kvall = _dot(ckvn, wkv_ref[...])
    for h in range(MLA_HEADS):
        o = h * 256
        q_ref[h] = jnp.concatenate([qall[:, o:o + 128], qall[:, o + 128:o + 256] * cs], axis=1).astype(BF16)
        k_ref[h] = jnp.concatenate([kvall[:, o:o + 128], krkr], axis=1).astype(BF16)
        v_ref[h] = kvall[:, o + 128:o + 256].astype(BF16)


def _mla_prep(a, pos_col, inv, qnw, kvnw, wq, wkv):
    T = a.shape[0]
    tm = 512
    row = lambda i: (i, 0)
    const = lambda i: (0, 0)
    hrow = lambda i: (0, i, 0)
    return pl.pallas_call(
        _mla_prep_kernel,
        grid=(T // tm,),
        in_specs=[pl.BlockSpec((tm, 512), row), pl.BlockSpec((tm, 1), row), pl.BlockSpec((1, LANES), const),
                  pl.BlockSpec((1, 256), const), pl.BlockSpec((1, 128), const),
                  pl.BlockSpec((256, 1024), const), pl.BlockSpec((128, 1024), const)],
        out_specs=[pl.BlockSpec((MLA_HEADS, tm, 256), hrow), pl.BlockSpec((MLA_HEADS, tm, 256), hrow),
                   pl.BlockSpec((MLA_HEADS, tm, 128), hrow)],
        out_shape=[jax.ShapeDtypeStruct((MLA_HEADS, T, 256), BF16), jax.ShapeDtypeStruct((MLA_HEADS, T, 256), BF16),
                   jax.ShapeDtypeStruct((MLA_HEADS, T, 128), BF16)],
        compiler_params=_cparams(("parallel",)),
        name="mla_prep",
    )(a, pos_col, inv, qnw, kvnw, wq, wkv)


FLASH_STRIP = LANES


def _flash_init(m_s, l_s, acc_s, p_s, a_s, s_s):
    del s_s
    m_s[...] = jnp.full(m_s.shape, -jnp.inf, F32)
    l_s[...] = jnp.zeros(l_s.shape, F32)
    acc_s[...] = jnp.zeros(acc_s.shape, F32)
    p_s[1] = jnp.zeros(p_s.shape[1:], BF16)
    a_s[1] = jnp.ones(a_s.shape[1:], F32)


def _flash_pv(v_t, slot, acc_s, p_s, a_s):
    acc_s[...] = a_s[slot] * acc_s[...] + _dot(v_t, p_s[slot])


def _flash_stage(next_scores, strip_fn, v_prev, cur, m_s, l_s, acc_s, p_s, a_s, s_s):
    if next_scores is not None:
        s_s[1 - cur] = next_scores()
    _flash_pv(v_prev, 1 - cur, acc_s, p_s, a_s)
    for j in range(m_s.shape[1] // FLASH_STRIP):
        sl = slice(j * FLASH_STRIP, (j + 1) * FLASH_STRIP)
        s, bias, visible = strip_fn(s_s, cur, j)
        keep = (lambda x: x) if visible is None else (lambda x: jnp.where(visible, x, -jnp.inf))
        m_old = m_s[:, sl]
        t = s if bias is None else s - bias
        m_new = jnp.maximum(m_old, jnp.max(keep(t), axis=0, keepdims=True))
        alpha = jnp.exp2(m_old - m_new)
        p = jnp.exp2(keep(t - m_new))
        l_s[:, sl] = alpha * l_s[:, sl] + jnp.sum(p, axis=0, keepdims=True)
        m_s[:, sl] = m_new
        p_s[cur, :, sl] = p.astype(BF16)
        a_s[cur, :, sl] = alpha


def _flash_run(qi, scores, stage, flush, s_s):
    if scores is not None:
        s_s[0] = scores(0)

    def pair(i, _):
        a = 2 * i
        stage(a + 1, a, False, 0)
        stage(a + 2, a + 1, False, 1)
        return 0

    lax.fori_loop(0, qi // 2, pair, 0)

    @pl.when(qi % 2 == 0)
    def _():
        stage(None, qi, True, 0)
        flush(qi, 0)

    @pl.when(qi % 2 == 1)
    def _():
        stage(qi, qi - 1, False, 0)
        stage(None, qi, True, 1)
        flush(qi, 1)


def _causal_strip(shape, j):
    r = lax.broadcasted_iota(jnp.int32, shape, 0)
    c = lax.broadcasted_iota(jnp.int32, shape, 1) + j * FLASH_STRIP
    return r <= c


def _flash_scratch(dv, tk, nq, score_slots=True):
    s_shape = (2, tk, nq) if score_slots else (2, SUBLANES, LANES)
    return [pltpu.VMEM((1, nq), F32), pltpu.VMEM((1, nq), F32), pltpu.VMEM((dv, nq), F32),
            pltpu.VMEM((2, tk, nq), BF16), pltpu.VMEM((2, 1, nq), F32), pltpu.VMEM(s_shape, F32)]


def _mla_flash_kernel(qt_ref, k_ref, vt_ref, o_ref, *state, tq):
    m_s, l_s, acc_s, p_s, a_s, s_s = state
    qi = pl.program_id(2)
    _flash_init(*state)
    q_t = qt_ref[...]

    def scores(kb):
        return _dot(k_ref[pl.ds(pl.multiple_of(kb * tq, tq), tq), :], q_t)

    def stage(nxt, kb, masked, cur):
        def strip(s_ref, slot, j):
            t = s_ref[slot, :, j * FLASH_STRIP:(j + 1) * FLASH_STRIP]
            return t, None, (_causal_strip(t.shape, j) if masked else None)

        _flash_stage(None if nxt is None else (lambda: scores(nxt)), strip, vt_ref[jnp.maximum(kb - 1, 0)], cur,
                     *state)

    _flash_run(qi, scores, stage, lambda kb, slot: _flash_pv(vt_ref[kb], slot, acc_s, p_s, a_s), s_s)
    o_ref[...] = (acc_s[...] * (1.0 / l_s[...])).T.astype(o_ref.dtype)


def _mla_flash(q_t, k, v_t, batch, seq, tq):
    nq = seq // tq
    return pl.pallas_call(
        functools.partial(_mla_flash_kernel, tq=tq),
        grid=(batch, MLA_HEADS, nq),
        in_specs=[pl.BlockSpec((None, 256, tq), lambda b, h, i: (h, 0, b * nq + i)),
                  pl.BlockSpec((None, seq, 256), lambda b, h, i: (h, b, 0)),
                  pl.BlockSpec((None, nq, MLA_D_V, tq), lambda b, h, i: (h, b, 0, 0))],
        out_specs=pl.BlockSpec((tq, MLA_D_V), lambda b, h, i: (b * nq + i, h)),
        out_shape=jax.ShapeDtypeStruct((batch * seq, MLA_HEADS * MLA_D_V), BF16),
        scratch_shapes=_flash_scratch(MLA_D_V, tq, tq),
        compiler_params=_cparams(("parallel", "parallel", "arbitrary")),
        name="mla_flash",
    )(q_t, k, v_t)


def _diff_flash_kernel(qt_ref, k_ref, vt_ref, pq_ref, pk_ref, slope_ref, lam_ref, sw_ref, o_ref,
                       m_s, l_s, acc_s, p_s, a_s, s_s, kp_s, *, tq, lam_init, ordered):
    state = (m_s, l_s, acc_s, p_s, a_s, s_s)
    qi = pl.program_id(2)
    _flash_init(*state)
    q_t = qt_ref[...]
    row = lax.broadcasted_iota(jnp.int32, q_t.shape, 0)
    zero = jnp.zeros_like(q_t)
    q2_t = jnp.concatenate([jnp.where(row < DIFF_D_QK, q_t, zero), jnp.where(row >= DIFF_D_QK, q_t, zero)], axis=1)
    spm = tq // FLASH_STRIP
    if ordered:
        ones = jnp.where(lax.broadcasted_iota(jnp.int32, q2_t.shape, 0) < 3, 1.0, 0.0).astype(BF16)
        q2_t = jnp.concatenate([q2_t, ones], axis=0)

        @pl.when(qi == 0)
        def _():
            lane = lax.broadcasted_iota(jnp.int32, (tq, LANES), 1)

            def fill(kb, _):
                pieces = _split_bf16(slope_ref[...] * pk_ref[kb], 3)
                tile = jnp.zeros((tq, LANES), F32)
                for i, piece in enumerate(pieces):
                    tile = jnp.where(lane == i, piece.astype(F32), tile)
                kp_s[pl.ds(pl.multiple_of(kb * tq, tq), tq), :] = tile.astype(BF16)
                return 0

            lax.fori_loop(0, pk_ref.shape[0], fill, 0)
    else:
        slope = slope_ref[...]
        spq = slope * pq_ref[...]

    def scores(kb):
        rows = pl.ds(pl.multiple_of(kb * tq, tq), tq)
        k = k_ref[rows, :]
        if ordered:
            k = jnp.concatenate([k, kp_s[rows, :]], axis=1)
        return _dot(k, q2_t)

    def stage(nxt, kb, masked, cur):
        del nxt
        s_t = scores(kb)
        if not ordered:
            bias = jnp.abs(slope * pk_ref[kb] - spq)

        def strip(s_ref, slot, j):
            jq = j % spm
            t = s_t[:, j * FLASH_STRIP:(j + 1) * FLASH_STRIP]
            return (t, None if ordered else bias[:, jq * FLASH_STRIP:(jq + 1) * FLASH_STRIP],
                    _causal_strip(t.shape, jq) if masked else None)

        _flash_stage(None, strip, vt_ref[jnp.maximum(kb - 1, 0)], cur, *state)

    _flash_run(qi, None, stage, lambda kb, slot: _flash_pv(vt_ref[kb], slot, acc_s, p_s, a_s), s_s)
    o2 = acc_s[...] * (1.0 / l_s[...])
    lf = lam_ref[...]
    lam = (jnp.exp(jnp.sum(lf[0:1] * lf[1:2], keepdims=True))
           - jnp.exp(jnp.sum(lf[2:3] * lf[3:4], keepdims=True)) + lam_init)
    out = o2[:, :tq] - lam * o2[:, tq:]
    out = out * lax.rsqrt(jnp.mean(out * out, axis=0, keepdims=True) + SUBLN_EPS) * sw_ref[...] * (1.0 - lam_init)
    o_ref[...] = out.T.astype(o_ref.dtype)


def _diff_flash(q_t, k, v_t, pos_row, pos_kcol, slopes, lam, sw_col, batch, seq, tq, lam_init, ordered):
    nq = seq // tq
    return pl.pallas_call(
        functools.partial(_diff_flash_kernel, tq=tq, lam_init=lam_init, ordered=ordered),
        grid=(batch, DIFF_HEADS, nq),
        in_specs=[pl.BlockSpec((128, tq), lambda b, h, i: (h, b * nq + i)),
                  pl.BlockSpec((seq, 128), lambda b, h, i: (b, h)),
                  pl.BlockSpec((nq, DIFF_D_V, tq), lambda b, h, i: (b, h, 0)),
                  pl.BlockSpec((None, 1, tq), lambda b, h, i: (b, 0, i)),
                  pl.BlockSpec((nq, tq, 1), lambda b, h, i: (b, 0, 0)),
                  pl.BlockSpec((None, 1, 1), lambda b, h, i: (h, 0, 0)),
                  pl.BlockSpec((4, DIFF_D_QK), lambda b, h, i: (0, 0)),
                  pl.BlockSpec((DIFF_D_V, 1), lambda b, h, i: (0, 0))],
        out_specs=pl.BlockSpec((tq, DIFF_D_V), lambda b, h, i: (b * nq + i, h)),
        out_shape=jax.ShapeDtypeStruct((batch * seq, DIFF_HEADS * DIFF_D_V), BF16),
        scratch_shapes=_flash_scratch(DIFF_D_V, tq, 2 * tq, score_slots=False)
        + [pltpu.VMEM((seq, LANES) if ordered else (SUBLANES * 2, LANES), BF16)],
        compiler_params=_cparams(("parallel", "parallel", "arbitrary")),
        name="diff_flash",
    )(q_t, k, v_t, pos_row, pos_kcol, slopes, lam, sw_col)


def _ssd_kernel(xbc_ref, z_ref, dt_ref, cw_ref, cb_ref, dtb_ref, alog_ref, dsk_ref, nw_ref, e_ref, o_ref,
                ext, state):
    L = SSD_CHUNK
    G, N = SSD_GROUPS, SSD_STATE
    GW = SSD_D_INNER // G

    @pl.when(pl.program_id(1) == 0)
    def _():
        ext[0:SUBLANES, :] = jnp.zeros((SUBLANES, SSD_CONV_DIM), F32)
        state[...] = jnp.zeros(state.shape, F32)

    u = xbc_ref[...].astype(F32)
    ext[SUBLANES:SUBLANES + L, :] = u
    acc = cb_ref[...] + cw_ref[SSD_CONV - 1:SSD_CONV, :] * u
    for k in range(SSD_CONV - 1):
        o = SUBLANES - (SSD_CONV - 1) + k
        acc = acc + cw_ref[k:k + 1, :] * ext[o:o + L, :]
    ext[0:SUBLANES, :] = u[L - SUBLANES:L, :]
    act = acc * _sigmoid(acc)
    xs = act[:, :SSD_D_INNER]
    bm = act[:, SSD_D_INNER:SSD_D_INNER + G * N]
    cm = act[:, SSD_D_INNER + G * N:]

    dtr = dt_ref[...] + dtb_ref[...]
    dtv = jnp.maximum(dtr, 0.0) + jnp.log1p(jnp.exp(-jnp.abs(dtr)))
    adt = dtv * (-jnp.exp(alog_ref[...]))
    r = lax.broadcasted_iota(jnp.int32, (L, L), 0)
    c = lax.broadcasted_iota(jnp.int32, (L, L), 1)
    tril = r >= c
    a_cs = _dot(tril.astype(F32), adt, precision=HIGHEST)
    a_cs_t = a_cs.T
    ea = jnp.exp(a_cs)
    dte = jnp.exp(a_cs[L - 1:L, :] - a_cs)
    stack = jnp.concatenate([dtv, dte, ea], axis=0)
    hi, lo = _split_bf16(stack, 2)
    ex = _dot(hi, e_ref[...]) + _dot(lo, e_ref[...])
    dt_e, dte_e, ea_e = ex[:L], ex[L:2 * L], ex[2 * L:]
    xdt = xs * dt_e
    xdt_b = xdt.astype(BF16)
    xw_b = (xdt * dte_e).astype(BF16)
    bb = bm.astype(BF16)
    cbf = cm.astype(BF16)
    lane_g = lax.broadcasted_iota(jnp.int32, (1, GW), 1)
    ys = []
    for g in range(G):
        bg = bb[:, g * N:(g + 1) * N]
        cg = cbf[:, g * N:(g + 1) * N]
        cb = _dot_nt(cg, bg)
        xg = xdt_b[:, g * GW:(g + 1) * GW]
        yd = jnp.zeros((L, GW), F32)
        for rr in range(SSD_HEADS // G):
            h = g * (SSD_HEADS // G) + rr
            seg = a_cs[:, h:h + 1] - a_cs_t[h:h + 1, :]
            dec = jnp.exp(jnp.where(tril, seg, -jnp.inf))
            in_head = (lane_g >= rr * SSD_HEAD_DIM) & (lane_g < (rr + 1) * SSD_HEAD_DIM)
            yd = yd + _dot((cb * dec).astype(BF16), jnp.where(in_head, xg, jnp.zeros_like(xg)))
        st_old = state[g]
        eg = ea_e[:, g * GW:(g + 1) * GW]
        yoff = _dot(cg, st_old.astype(BF16)) * eg
        bg_t = bm[:, g * N:(g + 1) * N].T.astype(BF16)
        state[g] = st_old * eg[L - 1:L, :] + _dot(bg_t, xw_b[:, g * GW:(g + 1) * GW])
        ys.append(yd + yoff)
    y = jnp.concatenate(ys, axis=1) + xs * dsk_ref[...]
    zf = z_ref[...].astype(F32)
    y = y * (zf * _sigmoid(zf))
    y = jnp.concatenate([_rms(y[:, g * GW:(g + 1) * GW], NORM_EPS) for g in range(G)], axis=1)
    o_ref[...] = (y * nw_ref[...]).astype(o_ref.dtype)


def _ssd(xbc, z, dt, cw, cb, dtb, alog, dsk, nw, emat, batch, seq):
    L = SSD_CHUNK
    nc = seq // L
    row = lambda b, c: (b * nc + c, 0)
    const = lambda b, c: (0, 0)
    return pl.pallas_call(
        _ssd_kernel,
        grid=(batch, nc),
        in_specs=[pl.BlockSpec((L, SSD_CONV_DIM), row), pl.BlockSpec((L, SSD_D_INNER), row),
                  pl.BlockSpec((L, LANES), row),
                  pl.BlockSpec((SSD_CONV, SSD_CONV_DIM), const), pl.BlockSpec((1, SSD_CONV_DIM), const),
                  pl.BlockSpec((1, LANES), const), pl.BlockSpec((1, LANES), const),
                  pl.BlockSpec((1, SSD_D_INNER), const), pl.BlockSpec((1, SSD_D_INNER), const),
                  pl.BlockSpec((LANES, SSD_D_INNER), const)],
        out_specs=pl.BlockSpec((L, SSD_D_INNER), row),
        out_shape=jax.ShapeDtypeStruct((batch * seq, SSD_D_INNER), BF16),
        scratch_shapes=[pltpu.VMEM((SUBLANES + L, SSD_CONV_DIM), F32),
                        pltpu.VMEM((SSD_GROUPS, SSD_STATE, SSD_D_INNER // SSD_GROUPS), F32)],
        compiler_params=_cparams(("arbitrary", "arbitrary")),
        name="ssd",
    )(xbc, z, dt, cw, cb, dtb, alog, dsk, nw, emat)


def _outproj_router_kernel(x_ref, ym_ref, yd_ref, ys_ref, g1_ref, sh_ref, sc_ref, nw_ref, wm_ref, wd_ref, ws_ref,
                           wr_ref, br_ref, xo_ref, h_ref, route_ref, cnt_ref, carry, wr_s):
    @pl.when(pl.program_id(0) == 0)
    def _():
        carry[...] = jnp.zeros(carry.shape, F32)
        wr_s[0], wr_s[1] = _split_bf16(wr_ref[...], 2)

    y = _dot(ym_ref[...], wm_ref[...]) + _dot(yd_ref[...], wd_ref[...]) + _dot(ys_ref[...], ws_ref[...])
    xn = x_ref[...] + g1_ref[...] * y
    xo_ref[...] = xn
    h = _rms(xn, NORM_EPS) * nw_ref[...] * (1.0 + sc_ref[...]) + sh_ref[...]
    h_ref[...] = h
    h_hi, h_lo = _split_bf16(h, 2)
    logits = _dot(h_hi, wr_s[0]) + (_dot(h_lo, wr_s[0]) + _dot(h_hi, wr_s[1])) + br_ref[...]
    tm = logits.shape[0]
    lane = lax.broadcasted_iota(jnp.int32, logits.shape, 1)
    big = jnp.int32(1 << 20)
    neg = -jnp.inf
    lg = jnp.where((lane >= N_EXPERTS) & (lane < N_EXPERTS + MOE_GROUPS), logits, neg)
    mg = jnp.max(lg, axis=-1, keepdims=True)
    pg_top = 1.0 / jnp.sum(jnp.exp(lg - mg), axis=-1, keepdims=True)
    gsel = jnp.min(jnp.where(lg == mg, lane, big), axis=-1, keepdims=True) - N_EXPERTS
    le = jnp.where((lane < N_EXPERTS) & ((lane // MOE_EXPERTS_PER_GROUP) == gsel), logits, neg)
    m1 = jnp.max(le, axis=-1, keepdims=True)
    e1 = jnp.min(jnp.where(le == m1, lane, big), axis=-1, keepdims=True)
    le2 = jnp.where(lane == e1, neg, le)
    m2 = jnp.max(le2, axis=-1, keepdims=True)
    e2 = jnp.min(jnp.where(le2 == m2, lane, big), axis=-1, keepdims=True)
    rr = jnp.exp(m2 - m1)
    gate1 = pg_top / (1.0 + rr)
    gate2 = pg_top * rr / (1.0 + rr)
    onehot = jnp.where((lane == e1) | (lane == e2), 1.0, 0.0)
    ri = lax.broadcasted_iota(jnp.int32, (tm, tm), 0)
    ci = lax.broadcasted_iota(jnp.int32, (tm, tm), 1)
    before = jnp.where(ri > ci, 1.0, 0.0).astype(BF16)
    prefix = _dot(before, onehot.astype(BF16)) + carry[...]
    rank1 = jnp.sum(jnp.where(lane == e1, prefix, 0.0), axis=-1, keepdims=True)
    rank2 = jnp.sum(jnp.where(lane == e2, prefix, 0.0), axis=-1, keepdims=True)
    carry[...] = carry[...] + jnp.sum(onehot, axis=0, keepdims=True)
    cnt_ref[...] = carry[...]
    cols = (e1.astype(F32), e2.astype(F32), rank1, rank2, gate1, gate2)
    route = jnp.zeros(logits.shape, F32)
    for i, col in enumerate(cols):
        route = jnp.where(lane == i, col, route)
    route_ref[...] = route


def _outproj_router(xf, ym, yd, ys, g1, sh2, sc2, n2w, wom, wod, wos, wr, br, seq):
    T, D = xf.shape
    tm = 256
    per_b = seq // tm
    row = lambda i: (i, 0)
    const = lambda i: (0, 0)
    bat = lambda i: (i // per_b, 0, 0)
    return pl.pallas_call(
        _outproj_router_kernel,
        grid=(T // tm,),
        in_specs=[pl.BlockSpec((tm, D), row), pl.BlockSpec((tm, 512), row), pl.BlockSpec((tm, 512), row),
                  pl.BlockSpec((tm, SSD_D_INNER), row),
                  pl.BlockSpec((None, 1, D), bat), pl.BlockSpec((None, 1, D), bat), pl.BlockSpec((None, 1, D), bat),
                  pl.BlockSpec((1, D), const),
                  pl.BlockSpec((512, D), const), pl.BlockSpec((512, D), const), pl.BlockSpec((SSD_D_INNER, D), const),
                  pl.BlockSpec((D, LANES), const), pl.BlockSpec((1, LANES), const)],
        out_specs=[pl.BlockSpec((tm, D), row), pl.BlockSpec((tm, D), row), pl.BlockSpec((tm, LANES), row),
                   pl.BlockSpec((1, LANES), const)],
        out_shape=[jax.ShapeDtypeStruct((T, D), F32), jax.ShapeDtypeStruct((T, D), F32),
                   jax.ShapeDtypeStruct((T, LANES), F32), jax.ShapeDtypeStruct((1, LANES), F32)],
        scratch_shapes=[pltpu.VMEM((1, LANES), F32), pltpu.VMEM((2, D, LANES), BF16)],
        compiler_params=_cparams(("arbitrary",)),
        name="outproj_router",
    )(xf, ym, yd, ys, g1, sh2, sc2, n2w, wom, wod, wos, wr, br)


def _row_copy(src, dst, i_src, i_dst, sem):
    return pltpu.make_async_copy(src.at[pl.ds(i_src, 1)], dst.at[pl.ds(i_dst, 1)], sem)


def _dispatch_kernel(dest_ref, h_ref, xb_in_ref, xb_ref, sem):
    del xb_in_ref
    tm = h_ref.shape[0]

    def issue(r, _):
        for k in range(2):
            _row_copy(h_ref, xb_ref, r, dest_ref[2 * r + k], sem).start()
        return 0

    lax.fori_loop(0, tm, issue, 0, unroll=8)

    def drain(r, _):
        for k in range(2):
            _row_copy(h_ref, xb_ref, 0, 0, sem).wait()
        return 0

    lax.fori_loop(0, tm, drain, 0, unroll=8)


def _dispatch(dest, h, xb_zero):
    T, D = h.shape
    tm = 256
    return pl.pallas_call(
        _dispatch_kernel,
        grid=(T // tm,),
        in_specs=[pl.BlockSpec((2 * tm,), lambda i: (i,), memory_space=pltpu.SMEM),
                  pl.BlockSpec((tm, D), lambda i: (i, 0)),
                  pl.BlockSpec(memory_space=pl.ANY)],
        out_specs=pl.BlockSpec(memory_space=pl.ANY),
        out_shape=jax.ShapeDtypeStruct(xb_zero.shape, F32),
        scratch_shapes=[pltpu.SemaphoreType.DMA(())],
        input_output_aliases={2: 0},
        compiler_params=_cparams(("arbitrary",)),
        name="moe_dispatch",
    )(dest, h, xb_zero)


def _expert_kernel(bexp_ref, nused_ref, x_ref, wg_ref, wu_ref, wd_ref, o_ref, wg_s, wu_s, wd_s):
    j = pl.program_id(0)
    used = j < nused_ref[0]
    new_expert = (j == 0) | (bexp_ref[j] != bexp_ref[jnp.maximum(j - 1, 0)])

    @pl.when(used & new_expert)
    def _():
        wg_s[...] = wg_ref[...].astype(BF16)
        wu_s[...] = wu_ref[...].astype(BF16)
        wd_s[...] = wd_ref[...].astype(BF16)

    @pl.when(used)
    def _():
        x = x_ref[...].astype(BF16)
        g = _dot(x, wg_s[...])
        u = _dot(x, wu_s[...])
        o_ref[...] = _dot((g * _sigmoid(g) * u).astype(BF16), wd_s[...])

    @pl.when(jnp.logical_not(used))
    def _():
        o_ref[...] = jnp.zeros(o_ref.shape, F32)


def _experts(bexp, nused, xb, w_gate, w_up, w_down, layer):
    cap, D = xb.shape
    nb = cap // MOE_ROWS
    F = D_EXPERT
    last = lambda j, be, nu: jnp.maximum(jnp.minimum(j, nu[0] - 1), 0)
    return pl.pallas_call(
        _expert_kernel,
        grid_spec=pltpu.PrefetchScalarGridSpec(
            num_scalar_prefetch=2,
            grid=(nb,),
            in_specs=[pl.BlockSpec((MOE_ROWS, D), lambda j, be, nu: (last(j, be, nu), 0)),
                      pl.BlockSpec((None, None, D, F), lambda j, be, nu: (layer, be[last(j, be, nu)], 0, 0)),
                      pl.BlockSpec((None, None, D, F), lambda j, be, nu: (layer, be[last(j, be, nu)], 0, 0)),
                      pl.BlockSpec((None, None, F, D), lambda j, be, nu: (layer, be[last(j, be, nu)], 0, 0))],
            out_specs=pl.BlockSpec((MOE_ROWS, D), lambda j, be, nu: (j, 0)),
            scratch_shapes=[pltpu.VMEM((D, F), BF16), pltpu.VMEM((D, F), BF16), pltpu.VMEM((F, D), BF16)],
        ),
        out_shape=jax.ShapeDtypeStruct((cap, D), F32),
        compiler_params=_cparams(("arbitrary",)),
        name="moe_experts",
    )(bexp, nused, xb, w_gate, w_up, w_down)


def _combine_kernel(dest_ref, dest_next_ref, x_ref, route_ref, g2_ref, fnw_ref, yb_ref, o_ref, gbuf, sem, *, final):
    tm = x_ref.shape[0]
    i = pl.program_id(0)
    slot = i % 2

    def issue(d_ref, s):
        def body(r, _):
            for k in range(2):
                _row_copy(yb_ref, gbuf.at[s, k], d_ref[2 * r + k], r, sem.at[s]).start()
            return 0

        lax.fori_loop(0, tm, body, 0, unroll=8)

    @pl.when(i == 0)
    def _():
        issue(dest_ref, 0)

    @pl.when(i + 1 < pl.num_programs(0))
    def _():
        issue(dest_next_ref, 1 - slot)

    def drain(r, _):
        for k in range(2):
            _row_copy(yb_ref, gbuf.at[slot, k], 0, 0, sem.at[slot]).wait()
        return 0

    lax.fori_loop(0, tm, drain, 0, unroll=8)
    route = route_ref[...]
    y = route[:, 4:5] * gbuf[slot, 0] + route[:, 5:6] * gbuf[slot, 1]
    xo = x_ref[...] + g2_ref[...] * y
    if final:
        xo = _rms(xo, NORM_EPS) * fnw_ref[...]
    o_ref[...] = xo


def _combine(dest, xf, route, g2, fnw, yb, seq, final):
    T, D = xf.shape
    tm = 256
    per_b = seq // tm
    last = T // tm - 1
    return pl.pallas_call(
        functools.partial(_combine_kernel, final=final),
        grid=(T // tm,),
        in_specs=[pl.BlockSpec((2 * tm,), lambda i: (i,), memory_space=pltpu.SMEM),
                  pl.BlockSpec((2 * tm,), lambda i: (jnp.minimum(i + 1, last),), memory_space=pltpu.SMEM),
                  pl.BlockSpec((tm, D), lambda i: (i, 0)),
                  pl.BlockSpec((tm, LANES), lambda i: (i, 0)),
                  pl.BlockSpec((None, 1, D), lambda i: (i // per_b, 0, 0)),
                  pl.BlockSpec((1, D), lambda i: (0, 0)),
                  pl.BlockSpec(memory_space=pl.ANY)],
        out_specs=pl.BlockSpec((tm, D), lambda i: (i, 0)),
        out_shape=jax.ShapeDtypeStruct((T, D), F32),
        scratch_shapes=[pltpu.VMEM((2, 2, tm, D), F32), pltpu.SemaphoreType.DMA((2,))],
        compiler_params=_cparams(("arbitrary",)),
        name="moe_combine",
    )(dest, dest, xf, route, g2, fnw, yb)


def _rot_cols(w):
    half = w.shape[1] // 2
    return jnp.concatenate([-w[:, half:], w[:, :half]], axis=1)


def _pad_lanes(a, n=LANES):
    return jnp.pad(a, [(0, 0)] * (a.ndim - 1) + [(0, n - a.shape[-1])])


def kernel(x, c, positions, w_ada, b_ada, norm1_w, w_in, mla_q_norm_w, mla_w_uq, mla_kv_norm_w, mla_w_ukv, diff_lambda, diff_subln_w, ssd_conv_w, ssd_conv_b, ssd_dt_bias, ssd_a_log, ssd_d, ssd_norm_w, w_out, norm2_w, router_w_group, router_b_group, router_w_expert, router_b_expert, exp_w_gate, exp_w_up, exp_w_down, final_norm_w):
    B, S, D = x.shape
    T = B * S
    xf = x.reshape(T, D)
    mod = _ada(jnp.pad(c, ((0, SUBLANES - B), (0, 0))), w_ada, b_ada)[:, :B]

    pos_f = positions.astype(F32)
    pos_col = pos_f.reshape(T, 1)
    inv = 1.0 / (ROPE_THETA ** (jnp.arange(0, MLA_D_ROPE, 2, dtype=F32) / MLA_D_ROPE))
    inv = jnp.tile(inv, 4).reshape(1, LANES)
    slopes = jnp.exp2(-8.0 / DIFF_HEADS * jnp.arange(1, DIFF_HEADS + 1, dtype=F32)).reshape(DIFF_HEADS, 1, 1)
    slopes = slopes * LOG2E
    tq = min(FLASH_TQ, S)
    tqd = min(FLASH_TQ_DIFF, S)
    pos_ordered = jnp.all(positions[:, 1:] >= positions[:, :-1])
    head_of_col = np.arange(SSD_D_INNER) // SSD_HEAD_DIM
    emat = jnp.asarray(np.arange(LANES)[:, None] == head_of_col[None, :], BF16)

    cap = T * 2 + N_EXPERTS * MOE_ROWS
    nb = cap // MOE_ROWS
    xb_zero = jnp.zeros((cap, D), F32)
    fnw = final_norm_w.reshape(1, D)

    for l in range(DEPTH):
        sh1, sc1, g1, sh2, sc2, g2 = [mod[l, :, i * D:(i + 1) * D].reshape(B, 1, D) for i in range(6)]
        w = w_in[l]
        ws = [jnp.concatenate([w[:, :_OFF[3]], _rot_cols(w[:, _OFF[2]:_OFF[3]])], axis=1)]
        ws.append(w[:, _OFF[3]:_OFF[4]] * (DIFF_D_QK ** -0.5 * LOG2E))
        ws += [w[:, _OFF[i]:_OFF[i + 1]] for i in range(4, 8)]
        ws.append(_pad_lanes(w[:, _OFF[8]:_OFF[9]]))
        ws = [a.astype(BF16) for a in ws]
        a_mla, dq, dk, dv, z, xbc, dt = _inproj(xf, norm1_w[l].reshape(1, D), sh1, sc1, ws, S)

        wq = mla_w_uq[l].reshape(MLA_Q_LORA, MLA_HEADS, MLA_D_NOPE + MLA_D_ROPE)
        wq_pe = wq[:, :, MLA_D_NOPE:]
        wq_rot = jnp.concatenate([-wq_pe[:, :, MLA_D_ROPE // 2:], wq_pe[:, :, :MLA_D_ROPE // 2]], axis=2)
        wq_cat = jnp.concatenate([wq, wq_rot], axis=2).reshape(MLA_Q_LORA, MLA_HEADS * 256).astype(BF16)
        q, k, v = _mla_prep(a_mla, pos_col, inv, mla_q_norm_w[l].reshape(1, -1), mla_kv_norm_w[l].reshape(1, -1),
                            wq_cat, mla_w_ukv[l].astype(BF16))
        q_t = q.transpose(0, 2, 1)
        v_t = v.reshape(MLA_HEADS, T // tq, tq, MLA_D_V).transpose(0, 1, 3, 2)
        y_mla = _mla_flash(q_t, k, v_t, B, S, tq)

        lam_init = 0.8 - 0.6 * math.exp(-0.3 * l)
        dv_t = dv.reshape(T // tqd, tqd, DIFF_HEADS * DIFF_D_V).transpose(0, 2, 1)
        diff_args = (dq.T, dk, dv_t, pos_f.reshape(B, 1, S), pos_f.reshape(T // tqd, tqd, 1), slopes,
                     diff_lambda[l], diff_subln_w[l].reshape(-1, 1))
        y_diff = lax.cond(pos_ordered,
                          lambda *a: _diff_flash(*a, B, S, tqd, lam_init, ordered=True),
                          lambda *a: _diff_flash(*a, B, S, tqd, lam_init, ordered=False), *diff_args)

        y_ssd = _ssd(xbc, z, dt, ssd_conv_w[l], ssd_conv_b[l].reshape(1, -1),
                     _pad_lanes(ssd_dt_bias[l].reshape(1, -1)), _pad_lanes(ssd_a_log[l].reshape(1, -1)),
                     jnp.repeat(ssd_d[l], SSD_HEAD_DIM).reshape(1, -1), ssd_norm_w[l].reshape(1, -1), emat, B, S)

        wo = w_out[l].astype(BF16)
        wr = _pad_lanes(jnp.concatenate([router_w_expert[l], router_w_group[l]], axis=1))
        br = _pad_lanes(jnp.concatenate([router_b_expert[l], router_b_group[l]]).reshape(1, -1))
        x_mid, h2, route, cnt = _outproj_router(xf, y_mla, y_diff, y_ssd, g1, sh2, sc2, norm2_w[l].reshape(1, D),
                                                wo[:512], wo[512:1024], wo[1024:], wr, br, S)

        counts = cnt[0, :N_EXPERTS].astype(jnp.int32)
        pcounts = (counts + MOE_ROWS - 1) // MOE_ROWS * MOE_ROWS
        pend = jnp.cumsum(pcounts)
        pstart = pend - pcounts
        eid = route[:, 0:2].astype(jnp.int32)
        start_of = jnp.sum(jnp.where(eid[..., None] == jnp.arange(N_EXPERTS), pstart, 0), axis=-1)
        dest = (start_of + route[:, 2:4].astype(jnp.int32)).reshape(-1)
        starts = jnp.arange(nb, dtype=jnp.int32) * MOE_ROWS
        bexp = jnp.minimum(jnp.sum(pend[None, :] <= starts[:, None], axis=1), N_EXPERTS - 1).astype(jnp.int32)
        nused = (pend[-1:] // MOE_ROWS).astype(jnp.int32)

        xb = _dispatch(dest, h2, xb_zero)
        yb = _experts(bexp, nused, xb, exp_w_gate, exp_w_up, exp_w_down, l)
        xf = _combine(dest, x_mid, route, g2, fnw, yb, S, final=(l == DEPTH - 1))
    return xf.reshape(B, S, D)
```

```python
import functools
import math

import numpy as np
import jax
import jax.numpy as jnp
from jax import lax
from jax.experimental import pallas as pl
from jax.experimental.pallas import tpu as pltpu

F32 = jnp.float32
BF16 = jnp.bfloat16
HIGHEST = lax.Precision.HIGHEST

D_MODEL = 1024
DEPTH = 2
MLA_HEADS = 4
MLA_Q_LORA = 256
MLA_KV_LORA = 128
MLA_D_NOPE = 128
MLA_D_ROPE = 64
MLA_D_V = 128
ROPE_THETA = 10000.0
DIFF_HEADS = 4
DIFF_D_QK = 64
DIFF_D_V = 128
SSD_HEADS = 16
SSD_HEAD_DIM = 64
SSD_D_INNER = 1024
SSD_GROUPS = 4
SSD_STATE = 128
SSD_CONV = 4
SSD_CHUNK = 128
SSD_CONV_DIM = SSD_D_INNER + 2 * SSD_GROUPS * SSD_STATE
MOE_GROUPS = 4
MOE_EXPERTS_PER_GROUP = 8
N_EXPERTS = 32
D_EXPERT = 512
NORM_EPS = 1e-6
SUBLN_EPS = 1e-5

LANES = 128
SUBLANES = 8
MOE_ROWS = 256
FLASH_TQ = 1024
FLASH_TQ_DIFF = 512
LOG2E = math.log2(math.e)
VMEM_LIMIT = 48 * 1024 * 1024

_OFF = np.cumsum([0, MLA_Q_LORA, MLA_KV_LORA, MLA_D_ROPE, 512, 512, 512, SSD_D_INNER, SSD_CONV_DIM, SSD_HEADS])


def _cparams(sem):
    return pltpu.CompilerParams(dimension_semantics=sem, vmem_limit_bytes=VMEM_LIMIT)


def _dot(a, b, **kw):
    return jnp.dot(a, b, preferred_element_type=F32, **kw)


def _dot_nt(a, b):
    return lax.dot_general(a, b, (((1,), (1,)), ((), ())), preferred_element_type=F32)


def _rms(x, eps):
    return x * lax.rsqrt(jnp.mean(x * x, axis=-1, keepdims=True) + eps)


def _sigmoid(x):
    return 1.0 / (1.0 + jnp.exp(-x))


def _split_bf16(x, n):
    pieces = []
    for _ in range(n):
        head = x.astype(BF16)
        pieces.append(head)
        x = x - head.astype(F32)
    return pieces


def _ada_kernel(c_ref, w_ref, b_ref, o_ref):
    c = c_ref[...]
    cond = c * _sigmoid(c)
    o_ref[...] = _dot(cond, w_ref[...], precision=HIGHEST) + b_ref[...]


def _ada(c_pad, w_ada, b_ada):
    L, D, N = w_ada.shape
    tn = 1024
    return pl.pallas_call(
        _ada_kernel,
        grid=(L, N // tn),
        in_specs=[
            pl.BlockSpec((SUBLANES, D), lambda l, j: (0, 0)),
            pl.BlockSpec((None, D, tn), lambda l, j: (l, 0, j)),
            pl.BlockSpec((None, 1, tn), lambda l, j: (l, 0, j)),
        ],
        out_specs=pl.BlockSpec((None, SUBLANES, tn), lambda l, j: (l, 0, j)),
        out_shape=jax.ShapeDtypeStruct((L, SUBLANES, N), F32),
        compiler_params=_cparams(("parallel", "parallel")),
        name="ada",
    )(c_pad, w_ada, b_ada.reshape(L, 1, N))


def _inproj_kernel(x_ref, nw_ref, sh_ref, sc_ref, wm_ref, wq_ref, wk_ref, wv_ref, wz_ref, wx_ref, wt_ref,
                   om_ref, oq_ref, ok_ref, ov_ref, oz_ref, ox_ref, ot_ref):
    x = x_ref[...]
    h = _rms(x, NORM_EPS) * nw_ref[...] * (1.0 + sc_ref[...]) + sh_ref[...]
    hb = h.astype(BF16)
    for w_ref, o_ref in ((wm_ref, om_ref), (wq_ref, oq_ref), (wk_ref, ok_ref), (wv_ref, ov_ref),
                         (wz_ref, oz_ref), (wx_ref, ox_ref), (wt_ref, ot_ref)):
        o_ref[...] = _dot(hb, w_ref[...]).astype(o_ref.dtype)


def _inproj(xf, nw, sh, sc, ws, seq):
    T, D = xf.shape
    tm = 256
    per_b = seq // tm
    widths = [w.shape[1] for w in ws]
    row = lambda i: (i, 0)
    const = lambda i: (0, 0)
    bat = lambda i: (i // per_b, 0, 0)
    in_specs = [pl.BlockSpec((tm, D), row), pl.BlockSpec((1, D), const),
                pl.BlockSpec((None, 1, D), bat), pl.BlockSpec((None, 1, D), bat)]
    in_specs += [pl.BlockSpec((D, n), const) for n in widths]
    out_dt = [BF16] * 6 + [F32]
    return pl.pallas_call(
        _inproj_kernel,
        grid=(T // tm,),
        in_specs=in_specs,
        out_specs=[pl.BlockSpec((tm, n), row) for n in widths],
        out_shape=[jax.ShapeDtypeStruct((T, n), dt) for n, dt in zip(widths, out_dt)],
        compiler_params=_cparams(("parallel",)),
        name="inproj",
    )(xf, nw, sh, sc, *ws)


def _mla_prep_kernel(a_ref, pos_ref, inv_ref, qnw_ref, kvnw_ref, wq_ref, wkv_ref, q_ref, k_ref, v_ref):
    a = a_ref[...].astype(F32)
    cqn = (_rms(a[:, :256], NORM_EPS) * qnw_ref[...]).astype(BF16)
    ckvn = (_rms(a[:, 256:384], NORM_EPS) * kvnw_ref[...]).astype(BF16)
    ang = pos_ref[...] * inv_ref[...]
    lane = lax.broadcasted_iota(jnp.int32, ang.shape, 1)
    cs = jnp.where(lane < MLA_D_ROPE, jnp.cos(ang), jnp.sin(ang))
    kp = a[:, 384:512] * cs
    krkr = kp + pltpu.roll(kp, MLA_D_ROPE, axis=1)
    scale = (MLA_D_NOPE + MLA_D_ROPE) ** -0.5 * LOG2E
    qall = _dot(cqn, wq_ref[...]) * scale
    kvall = _dot(ckvn, wkv_ref[...])
    for h in range(MLA_HEADS):
        o = h * 256
        q_ref[h] = jnp.concatenate([qall[:, o:o + 128], qall[:, o + 128:o + 256] * cs], axis=1).astype(BF16)
        k_ref[h] = jnp.concatenate([kvall[:, o:o + 128], krkr], axis=1).astype(BF16)
        v_ref[h] = kvall[:, o + 128:o + 256].astype(BF16)


def _mla_prep(a, pos_col, inv, qnw, kvnw, wq, wkv):
    T = a.shape[0]
    tm = 512
    row = lambda i: (i, 0)
    const = lambda i: (0, 0)
    hrow = lambda i: (0, i, 0)
    return pl.pallas_call(
        _mla_prep_kernel,
        grid=(T // tm,),
        in_specs=[pl.BlockSpec((tm, 512), row), pl.BlockSpec((tm, 1), row), pl.BlockSpec((1, LANES), const),
                  pl.BlockSpec((1, 256), const), pl.BlockSpec((1, 128), const),
                  pl.BlockSpec((256, 1024), const), pl.BlockSpec((128, 1024), const)],
        out_specs=[pl.BlockSpec((MLA_HEADS, tm, 256), hrow), pl.BlockSpec((MLA_HEADS, tm, 256), hrow),
                   pl.BlockSpec((MLA_HEADS, tm, 128), hrow)],
        out_shape=[jax.ShapeDtypeStruct((MLA_HEADS, T, 256), BF16), jax.ShapeDtypeStruct((MLA_HEADS, T, 256), BF16),
                   jax.ShapeDtypeStruct((MLA_HEADS, T, 128), BF16)],
        compiler_params=_cparams(("parallel",)),
        name="mla_prep",
    )(a, pos_col, inv, qnw, kvnw, wq, wkv)


FLASH_STRIP = LANES


def _flash_init(m_s, l_s, acc_s, p_s, a_s, s_s):
    del s_s
    m_s[...] = jnp.full(m_s.shape, -jnp.inf, F32)
    l_s[...] = jnp.zeros(l_s.shape, F32)
    acc_s[...] = jnp.zeros(acc_s.shape, F32)
    p_s[1] = jnp.zeros(p_s.shape[1:], BF16)
    a_s[1] = jnp.ones(a_s.shape[1:], F32)


def _flash_pv(v_t, slot, acc_s, p_s, a_s):
    acc_s[...] = a_s[slot] * acc_s[...] + _dot(v_t, p_s[slot])


def _flash_stage(next_scores, strip_fn, v_prev, cur, m_s, l_s, acc_s, p_s, a_s, s_s):
    if next_scores is not None:
        s_s[1 - cur] = next_scores()
    _flash_pv(v_prev, 1 - cur, acc_s, p_s, a_s)
    for j in range(m_s.shape[1] // FLASH_STRIP):
        sl = slice(j * FLASH_STRIP, (j + 1) * FLASH_STRIP)
        s, bias, visible = strip_fn(s_s, cur, j)
        keep = (lambda x: x) if visible is None else (lambda x: jnp.where(visible, x, -jnp.inf))
        m_old = m_s[:, sl]
        t = s if bias is None else s - bias
        m_new = jnp.maximum(m_old, jnp.max(keep(t), axis=0, keepdims=True))
        alpha = jnp.exp2(m_old - m_new)
        p = jnp.exp2(keep(t - m_new))
        l_s[:, sl] = alpha * l_s[:, sl] + jnp.sum(p, axis=0, keepdims=True)
        m_s[:, sl] = m_new
        p_s[cur, :, sl] = p.astype(BF16)
        a_s[cur, :, sl] = alpha


def _flash_run(qi, scores, stage, flush, s_s):
    if scores is not None:
        s_s[0] = scores(0)

    def pair(i, _):
        a = 2 * i
        stage(a + 1, a, False, 0)
        stage(a + 2, a + 1, False, 1)
        return 0

    lax.fori_loop(0, qi // 2, pair, 0)

    @pl.when(qi % 2 == 0)
    def _():
        stage(None, qi, True, 0)
        flush(qi, 0)

    @pl.when(qi % 2 == 1)
    def _():
        stage(qi, qi - 1, False, 0)
        stage(None, qi, True, 1)
        flush(qi, 1)


def _causal_strip(shape, j):
    r = lax.broadcasted_iota(jnp.int32, shape, 0)
    c = lax.broadcasted_iota(jnp.int32, shape, 1) + j * FLASH_STRIP
    return r <= c


def _flash_scratch(dv, tk, nq, score_slots=True):
    s_shape = (2, tk, nq) if score_slots else (2, SUBLANES, LANES)
    return [pltpu.VMEM((1, nq), F32), pltpu.VMEM((1, nq), F32), pltpu.VMEM((dv, nq), F32),
            pltpu.VMEM((2, tk, nq), BF16), pltpu.VMEM((2, 1, nq), F32), pltpu.VMEM(s_shape, F32)]


def _mla_flash_kernel(qt_ref, k_ref, vt_ref, o_ref, *state, tq):
    m_s, l_s, acc_s, p_s, a_s, s_s = state
    qi = pl.program_id(2)
    _flash_init(*state)
    q_t = qt_ref[...]

    def scores(kb):
        return _dot(k_ref[pl.ds(pl.multiple_of(kb * tq, tq), tq), :], q_t)

    def stage(nxt, kb, masked, cur):
        def strip(s_ref, slot, j):
            t = s_ref[slot, :, j * FLASH_STRIP:(j + 1) * FLASH_STRIP]
            return t, None, (_causal_strip(t.shape, j) if masked else None)

        _flash_stage(None if nxt is None else (lambda: scores(nxt)), strip, vt_ref[jnp.maximum(kb - 1, 0)], cur,
                     *state)

    _flash_run(qi, scores, stage, lambda kb, slot: _flash_pv(vt_ref[kb], slot, acc_s, p_s, a_s), s_s)
    o_ref[...] = (acc_s[...] * (1.0 / l_s[...])).T.astype(o_ref.dtype)


def _mla_flash(q_t, k, v_t, batch, seq, tq):
    nq = seq // tq
    return pl.pallas_call(
        functools.partial(_mla_flash_kernel, tq=tq),
        grid=(batch, MLA_HEADS, nq),
        in_specs=[pl.BlockSpec((None, 256, tq), lambda b, h, i: (h, 0, b * nq + i)),
                  pl.BlockSpec((None, seq, 256), lambda b, h, i: (h, b, 0)),
                  pl.BlockSpec((None, nq, MLA_D_V, tq), lambda b, h, i: (h, b, 0, 0))],
        out_specs=pl.BlockSpec((tq, MLA_D_V), lambda b, h, i: (b * nq + i, h)),
        out_shape=jax.ShapeDtypeStruct((batch * seq, MLA_HEADS * MLA_D_V), BF16),
        scratch_shapes=_flash_scratch(MLA_D_V, tq, tq),
        compiler_params=_cparams(("parallel", "parallel", "arbitrary")),
        name="mla_flash",
    )(q_t, k, v_t)


def _diff_flash_kernel(qt_ref, k_ref, vt_ref, pq_ref, pk_ref, slope_ref, lam_ref, sw_ref, o_ref,
                       m_s, l_s, acc_s, p_s, a_s, s_s, kp_s, *, tq, lam_init, ordered):
    state = (m_s, l_s, acc_s, p_s, a_s, s_s)
    qi = pl.program_id(2)
    _flash_init(*state)
    q_t = qt_ref[...]
    row = lax.broadcasted_iota(jnp.int32, q_t.shape, 0)
    zero = jnp.zeros_like(q_t)
    q2_t = jnp.concatenate([jnp.where(row < DIFF_D_QK, q_t, zero), jnp.where(row >= DIFF_D_QK, q_t, zero)], axis=1)
    spm = tq // FLASH_STRIP
    if ordered:
        ones = jnp.where(lax.broadcasted_iota(jnp.int32, q2_t.shape, 0) < 3, 1.0, 0.0).astype(BF16)
        q2_t = jnp.concatenate([q2_t, ones], axis=0)

        @pl.when(qi == 0)
        def _():
            lane = lax.broadcasted_iota(jnp.int32, (tq, LANES), 1)

            def fill(kb, _):
                pieces = _split_bf16(slope_ref[...] * pk_ref[kb], 3)
                tile = jnp.zeros((tq, LANES), F32)
                for i, piece in enumerate(pieces):
                    tile = jnp.where(lane == i, piece.astype(F32), tile)
                kp_s[pl.ds(pl.multiple_of(kb * tq, tq), tq), :] = tile.astype(BF16)
                return 0

            lax.fori_loop(0, pk_ref.shape[0], fill, 0)
    else:
        slope = slope_ref[...]
        spq = slope * pq_ref[...]

    def scores(kb):
        rows = pl.ds(pl.multiple_of(kb * tq, tq), tq)
        k = k_ref[rows, :]
        if ordered:
            k = jnp.concatenate([k, kp_s[rows, :]], axis=1)
        return _dot(k, q2_t)

    def stage(nxt, kb, masked, cur):
        del nxt
        s_t = scores(kb)
        if not ordered:
            bias = jnp.abs(slope * pk_ref[kb] - spq)

        def strip(s_ref, slot, j):
            jq = j % spm
            t = s_t[:, j * FLASH_STRIP:(j + 1) * FLASH_STRIP]
            return (t, None if ordered else bias[:, jq * FLASH_STRIP:(jq + 1) * FLASH_STRIP],
                    _causal_strip(t.shape, jq) if masked else None)

        _flash_stage(None, strip, vt_ref[jnp.maximum(kb - 1, 0)], cur, *state)

    _flash_run(qi, None, stage, lambda kb, slot: _flash_pv(vt_ref[kb], slot, acc_s, p_s, a_s), s_s)
    o2 = acc_s[...] * (1.0 / l_s[...])
    lf = lam_ref[...]
    lam = (jnp.exp(jnp.sum(lf[0:1] * lf[1:2], keepdims=True))
           - jnp.exp(jnp.sum(lf[2:3] * lf[3:4], keepdims=True)) + lam_init)
    out = o2[:, :tq] - lam * o2[:, tq:]
    out = out * lax.rsqrt(jnp.mean(out * out, axis=0, keepdims=True) + SUBLN_EPS) * sw_ref[...] * (1.0 - lam_init)
    o_ref[...] = out.T.astype(o_ref.dtype)


def _diff_flash(q_t, k, v_t, pos_row, pos_kcol, slopes, lam, sw_col, batch, seq, tq, lam_init, ordered):
    nq = seq // tq
    return pl.pallas_call(
        functools.partial(_diff_flash_kernel, tq=tq, lam_init=lam_init, ordered=ordered),
        grid=(batch, DIFF_HEADS, nq),
        in_specs=[pl.BlockSpec((128, tq), lambda b, h, i: (h, b * nq + i)),
                  pl.BlockSpec((seq, 128), lambda b, h, i: (b, h)),
                  pl.BlockSpec((nq, DIFF_D_V, tq), lambda b, h, i: (b, h, 0)),
                  pl.BlockSpec((None, 1, tq), lambda b, h, i: (b, 0, i)),
                  pl.BlockSpec((nq, tq, 1), lambda b, h, i: (b, 0, 0)),
                  pl.BlockSpec((None, 1, 1), lambda b, h, i: (h, 0, 0)),
                  pl.BlockSpec((4, DIFF_D_QK), lambda b, h, i: (0, 0)),
                  pl.BlockSpec((DIFF_D_V, 1), lambda b, h, i: (0, 0))],
        out_specs=pl.BlockSpec((tq, DIFF_D_V), lambda b, h, i: (b * nq + i, h)),
        out_shape=jax.ShapeDtypeStruct((batch * seq, DIFF_HEADS * DIFF_D_V), BF16),
        scratch_shapes=_flash_scratch(DIFF_D_V, tq, 2 * tq, score_slots=False)
        + [pltpu.VMEM((seq, LANES) if ordered else (SUBLANES * 2, LANES), BF16)],
        compiler_params=_cparams(("parallel", "parallel", "arbitrary")),
        name="diff_flash",
    )(q_t, k, v_t, pos_row, pos_kcol, slopes, lam, sw_col)


def _ssd_kernel(xbc_ref, z_ref, dt_ref, cw_ref, cb_ref, dtb_ref, alog_ref, dsk_ref, nw_ref, e_ref, o_ref,
                ext, state):
    L = SSD_CHUNK
    G, N = SSD_GROUPS, SSD_STATE
    GW = SSD_D_INNER // G

    @pl.when(pl.program_id(1) == 0)
    def _():
        ext[0:SUBLANES, :] = jnp.zeros((SUBLANES, SSD_CONV_DIM), F32)
        state[...] = jnp.zeros(state.shape, F32)

    u = xbc_ref[...].astype(F32)
    ext[SUBLANES:SUBLANES + L, :] = u
    acc = cb_ref[...] + cw_ref[SSD_CONV - 1:SSD_CONV, :] * u
    for k in range(SSD_CONV - 1):
        o = SUBLANES - (SSD_CONV - 1) + k
        acc = acc + cw_ref[k:k + 1, :] * ext[o:o + L, :]
    ext[0:SUBLANES, :] = u[L - SUBLANES:L, :]
    act = acc * _sigmoid(acc)
    xs = act[:, :SSD_D_INNER]
    bm = act[:, SSD_D_INNER:SSD_D_INNER + G * N]
    cm = act[:, SSD_D_INNER + G * N:]

    dtr = dt_ref[...] + dtb_ref[...]
    dtv = jnp.maximum(dtr, 0.0) + jnp.log1p(jnp.exp(-jnp.abs(dtr)))
    adt = dtv * (-jnp.exp(alog_ref[...]))
    r = lax.broadcasted_iota(jnp.int32, (L, L), 0)
    c = lax.broadcasted_iota(jnp.int32, (L, L), 1)
    tril = r >= c
    a_cs = _dot(tril.astype(F32), adt, precision=HIGHEST)
    a_cs_t = a_cs.T
    ea = jnp.exp(a_cs)
    dte = jnp.exp(a_cs[L - 1:L, :] - a_cs)
    stack = jnp.concatenate([dtv, dte, ea], axis=0)
    hi, lo = _split_bf16(stack, 2)
    ex = _dot(hi, e_ref[...]) + _dot(lo, e_ref[...])
    dt_e, dte_e, ea_e = ex[:L], ex[L:2 * L], ex[2 * L:]
    xdt = xs * dt_e
    xdt_b = xdt.astype(BF16)
    xw_b = (xdt * dte_e).astype(BF16)
    bb = bm.astype(BF16)
    cbf = cm.astype(BF16)
    lane_g = lax.broadcasted_iota(jnp.int32, (1, GW), 1)
    ys = []
    for g in range(G):
        bg = bb[:, g * N:(g + 1) * N]
        cg = cbf[:, g * N:(g + 1) * N]
        cb = _dot_nt(cg, bg)
        xg = xdt_b[:, g * GW:(g + 1) * GW]
        yd = jnp.zeros((L, GW), F32)
        for rr in range(SSD_HEADS // G):
            h = g * (SSD_HEADS // G) + rr
            seg = a_cs[:, h:h + 1] - a_cs_t[h:h + 1, :]
            dec = jnp.exp(jnp.where(tril, seg, -jnp.inf))
            in_head = (lane_g >= rr * SSD_HEAD_DIM) & (lane_g < (rr + 1) * SSD_HEAD_DIM)
            yd = yd + _dot((cb * dec).astype(BF16), jnp.where(in_head, xg, jnp.zeros_like(xg)))
        st_old = state[g]
        eg = ea_e[:, g * GW:(g + 1) * GW]
        yoff = _dot(cg, st_old.astype(BF16)) * eg
        bg_t = bm[:, g * N:(g + 1) * N].T.astype(BF16)
        state[g] = st_old * eg[L - 1:L, :] + _dot(bg_t, xw_b[:, g * GW:(g + 1) * GW])
        ys.append(yd + yoff)
    y = jnp.concatenate(ys, axis=1) + xs * dsk_ref[...]
    zf = z_ref[...].astype(F32)
    y = y * (zf * _sigmoid(zf))
    y = jnp.concatenate([_rms(y[:, g * GW:(g + 1) * GW], NORM_EPS) for g in range(G)], axis=1)
    o_ref[...] = (y * nw_ref[...]).astype(o_ref.dtype)


def _ssd(xbc, z, dt, cw, cb, dtb, alog, dsk, nw, emat, batch, seq):
    L = SSD_CHUNK
    nc = seq // L
    row = lambda b, c: (b * nc + c, 0)
    const = lambda b, c: (0, 0)
    return pl.pallas_call(
        _ssd_kernel,
        grid=(batch, nc),
        in_specs=[pl.BlockSpec((L, SSD_CONV_DIM), row), pl.BlockSpec((L, SSD_D_INNER), row),
                  pl.BlockSpec((L, LANES), row),
                  pl.BlockSpec((SSD_CONV, SSD_CONV_DIM), const), pl.BlockSpec((1, SSD_CONV_DIM), const),
                  pl.BlockSpec((1, LANES), const), pl.BlockSpec((1, LANES), const),
                  pl.BlockSpec((1, SSD_D_INNER), const), pl.BlockSpec((1, SSD_D_INNER), const),
                  pl.BlockSpec((LANES, SSD_D_INNER), const)],
        out_specs=pl.BlockSpec((L, SSD_D_INNER), row),
        out_shape=jax.ShapeDtypeStruct((batch * seq, SSD_D_INNER), BF16),
        scratch_shapes=[pltpu.VMEM((SUBLANES + L, SSD_CONV_DIM), F32),
                        pltpu.VMEM((SSD_GROUPS, SSD_STATE, SSD_D_INNER // SSD_GROUPS), F32)],
        compiler_params=_cparams(("arbitrary", "arbitrary")),
        name="ssd",
    )(xbc, z, dt, cw, cb, dtb, alog, dsk, nw, emat)


def _outproj_router_kernel(x_ref, ym_ref, yd_ref, ys_ref, g1_ref, sh_ref, sc_ref, nw_ref, wm_ref, wd_ref, ws_ref,
                           wr_ref, br_ref, xo_ref, h_ref, route_ref, cnt_ref, carry, wr_s):
    @pl.when(pl.program_id(0) == 0)
    def _():
        carry[...] = jnp.zeros(carry.shape, F32)
        wr_s[0], wr_s[1] = _split_bf16(wr_ref[...], 2)

    y = _dot(ym_ref[...], wm_ref[...]) + _dot(yd_ref[...], wd_ref[...]) + _dot(ys_ref[...], ws_ref[...])
    xn = x_ref[...] + g1_ref[...] * y
    xo_ref[...] = xn
    h = _rms(xn, NORM_EPS) * nw_ref[...] * (1.0 + sc_ref[...]) + sh_ref[...]
    h_ref[...] = h
    h_hi, h_lo = _split_bf16(h, 2)
    logits = _dot(h_hi, wr_s[0]) + (_dot(h_lo, wr_s[0]) + _dot(h_hi, wr_s[1])) + br_ref[...]
    tm = logits.shape[0]
    lane = lax.broadcasted_iota(jnp.int32, logits.shape, 1)
    big = jnp.int32(1 << 20)
    neg = -jnp.inf
    lg = jnp.where((lane >= N_EXPERTS) & (lane < N_EXPERTS + MOE_GROUPS), logits, neg)
    mg = jnp.max(lg, axis=-1, keepdims=True)
    pg_top = 1.0 / jnp.sum(jnp.exp(lg - mg), axis=-1, keepdims=True)
    gsel = jnp.min(jnp.where(lg == mg, lane, big), axis=-1, keepdims=True) - N_EXPERTS
    le = jnp.where((lane < N_EXPERTS) & ((lane // MOE_EXPERTS_PER_GROUP) == gsel), logits, neg)
    m1 = jnp.max(le, axis=-1, keepdims=True)
    e1 = jnp.min(jnp.where(le == m1, lane, big), axis=-1, keepdims=True)
    le2 = jnp.where(lane == e1, neg, le)
    m2 = jnp.max(le2, axis=-1, keepdims=True)
    e2 = jnp.min(jnp.where(le2 == m2, lane, big), axis=-1, keepdims=True)
    rr = jnp.exp(m2 - m1)
    gate1 = pg_top / (1.0 + rr)
    gate2 = pg_top * rr / (1.0 + rr)
    onehot = jnp.where((lane == e1) | (lane == e2), 1.0, 0.0)
    ri = lax.broadcasted_iota(jnp.int32, (tm, tm), 0)
    ci = lax.broadcasted_iota(jnp.int32, (tm, tm), 1)
    before = jnp.where(ri > ci, 1.0, 0.0).astype(BF16)
    prefix = _dot(before, onehot.astype(BF16)) + carry[...]
    rank1 = jnp.sum(jnp.where(lane == e1, prefix, 0.0), axis=-1, keepdims=True)
    rank2 = jnp.sum(jnp.where(lane == e2, prefix, 0.0), axis=-1, keepdims=True)
    carry[...] = carry[...] + jnp.sum(onehot, axis=0, keepdims=True)
    cnt_ref[...] = carry[...]
    cols = (e1.astype(F32), e2.astype(F32), rank1, rank2, gate1, gate2)
    route = jnp.zeros(logits.shape, F32)
    for i, col in enumerate(cols):
        route = jnp.where(lane == i, col, route)
    route_ref[...] = route


def _outproj_router(xf, ym, yd, ys, g1, sh2, sc2, n2w, wom, wod, wos, wr, br, seq):
    T, D = xf.shape
    tm = 512
    per_b = seq // tm
    row = lambda i: (i, 0)
    const = lambda i: (0, 0)
    bat = lambda i: (i // per_b, 0, 0)
    return pl.pallas_call(
        _outproj_router_kernel,
        grid=(T // tm,),
        in_specs=[pl.BlockSpec((tm, D), row), pl.BlockSpec((tm, 512), row), pl.BlockSpec((tm, 512), row),
                  pl.BlockSpec((tm, SSD_D_INNER), row),
                  pl.BlockSpec((None, 1, D), bat), pl.BlockSpec((None, 1, D), bat), pl.BlockSpec((None, 1, D), bat),
                  pl.BlockSpec((1, D), const),
                  pl.BlockSpec((512, D), const), pl.BlockSpec((512, D), const), pl.BlockSpec((SSD_D_INNER, D), const),
                  pl.BlockSpec((D, LANES), const), pl.BlockSpec((1, LANES), const)],
        out_specs=[pl.BlockSpec((tm, D), row), pl.BlockSpec((tm, D), row), pl.BlockSpec((tm, LANES), row),
                   pl.BlockSpec((1, LANES), const)],
        out_shape=[jax.ShapeDtypeStruct((T, D), F32), jax.ShapeDtypeStruct((T, D), F32),
                   jax.ShapeDtypeStruct((T, LANES), F32), jax.ShapeDtypeStruct((1, LANES), F32)],
        scratch_shapes=[pltpu.VMEM((1, LANES), F32), pltpu.VMEM((2, D, LANES), BF16)],
        compiler_params=_cparams(("arbitrary",)),
        name="outproj_router",
    )(xf, ym, yd, ys, g1, sh2, sc2, n2w, wom, wod, wos, wr, br)


def _row_copy(src, dst, i_src, i_dst, sem):
    return pltpu.make_async_copy(src.at[pl.ds(i_src, 1)], dst.at[pl.ds(i_dst, 1)], sem)


def _dispatch_kernel(dest_ref, h_ref, xb_in_ref, xb_ref, sem):
    del xb_in_ref
    tm = h_ref.shape[0]

    def issue(r, _):
        for k in range(2):
            _row_copy(h_ref, xb_ref, r, dest_ref[2 * r + k], sem).start()
        return 0

    lax.fori_loop(0, tm, issue, 0, unroll=8)

    def drain(r, _):
        for k in range(2):
            _row_copy(h_ref, xb_ref, 0, 0, sem).wait()
        return 0

    lax.fori_loop(0, tm, drain, 0, unroll=8)


def _dispatch(dest, h, xb_zero):
    T, D = h.shape
    tm = 256
    return pl.pallas_call(
        _dispatch_kernel,
        grid=(T // tm,),
        in_specs=[pl.BlockSpec((2 * tm,), lambda i: (i,), memory_space=pltpu.SMEM),
                  pl.BlockSpec((tm, D), lambda i: (i, 0)),
                  pl.BlockSpec(memory_space=pl.ANY)],
        out_specs=pl.BlockSpec(memory_space=pl.ANY),
        out_shape=jax.ShapeDtypeStruct(xb_zero.shape, F32),
        scratch_shapes=[pltpu.SemaphoreType.DMA(())],
        input_output_aliases={2: 0},
        compiler_params=_cparams(("arbitrary",)),
        name="moe_dispatch",
    )(dest, h, xb_zero)


def _expert_kernel(bexp_ref, nused_ref, x_ref, wg_ref, wu_ref, wd_ref, o_ref, wg_s, wu_s, wd_s):
    j = pl.program_id(0)
    used = j < nused_ref[0]
    new_expert = (j == 0) | (bexp_ref[j] != bexp_ref[jnp.maximum(j - 1, 0)])

    @pl.when(used & new_expert)
    def _():
        wg_s[...] = wg_ref[...].astype(BF16)
        wu_s[...] = wu_ref[...].astype(BF16)
        wd_s[...] = wd_ref[...].astype(BF16)

    @pl.when(used)
    def _():
        x = x_ref[...].astype(BF16)
        g = _dot(x, wg_s[...])
        u = _dot(x, wu_s[...])
        o_ref[...] = _dot((g * _sigmoid(g) * u).astype(BF16), wd_s[...])

    @pl.when(jnp.logical_not(used))
    def _():
        o_ref[...] = jnp.zeros(o_ref.shape, F32)


def _experts(bexp, nused, xb, w_gate, w_up, w_down, layer):
    cap, D = xb.shape
    nb = cap // MOE_ROWS
    F = D_EXPERT
    last = lambda j, be, nu: jnp.maximum(jnp.minimum(j, nu[0] - 1), 0)
    return pl.pallas_call(
        _expert_kernel,
        grid_spec=pltpu.PrefetchScalarGridSpec(
            num_scalar_prefetch=2,
            grid=(nb,),
            in_specs=[pl.BlockSpec((MOE_ROWS, D), lambda j, be, nu: (last(j, be, nu), 0)),
                      pl.BlockSpec((None, None, D, F), lambda j, be, nu: (layer, be[last(j, be, nu)], 0, 0)),
                      pl.BlockSpec((None, None, D, F), lambda j, be, nu: (layer, be[last(j, be, nu)], 0, 0)),
                      pl.BlockSpec((None, None, F, D), lambda j, be, nu: (layer, be[last(j, be, nu)], 0, 0))],
            out_specs=pl.BlockSpec((MOE_ROWS, D), lambda j, be, nu: (j, 0)),
            scratch_shapes=[pltpu.VMEM((D, F), BF16), pltpu.VMEM((D, F), BF16), pltpu.VMEM((F, D), BF16)],
        ),
        out_shape=jax.ShapeDtypeStruct((cap, D), F32),
        compiler_params=_cparams(("arbitrary",)),
        name="moe_experts",
    )(bexp, nused, xb, w_gate, w_up, w_down)


def _combine_kernel(dest_ref, dest_next_ref, x_ref, route_ref, g2_ref, fnw_ref, yb_ref, o_ref, gbuf, sem, *, final):
    tm = x_ref.shape[0]
    i = pl.program_id(0)
    slot = i % 2

    def issue(d_ref, s):
        def body(r, _):
            for k in range(2):
                _row_copy(yb_ref, gbuf.at[s, k], d_ref[2 * r + k], r, sem.at[s]).start()
            return 0

        lax.fori_loop(0, tm, body, 0, unroll=8)

    @pl.when(i == 0)
    def _():
        issue(dest_ref, 0)

    @pl.when(i + 1 < pl.num_programs(0))
    def _():
        issue(dest_next_ref, 1 - slot)

    def drain(r, _):
        for k in range(2):
            _row_copy(yb_ref, gbuf.at[slot, k], 0, 0, sem.at[slot]).wait()
        return 0

    lax.fori_loop(0, tm, drain, 0, unroll=8)
    route = route_ref[...]
    y = route[:, 4:5] * gbuf[slot, 0] + route[:, 5:6] * gbuf[slot, 1]
    xo = x_ref[...] + g2_ref[...] * y
    if final:
        xo = _rms(xo, NORM_EPS) * fnw_ref[...]
    o_ref[...] = xo


def _combine(dest, xf, route, g2, fnw, yb, seq, final):
    T, D = xf.shape
    tm = 256
    per_b = seq // tm
    last = T // tm - 1
    return pl.pallas_call(
        functools.partial(_combine_kernel, final=final),
        grid=(T // tm,),
        in_specs=[pl.BlockSpec((2 * tm,), lambda i: (i,), memory_space=pltpu.SMEM),
                  pl.BlockSpec((2 * tm,), lambda i: (jnp.minimum(i + 1, last),), memory_space=pltpu.SMEM),
                  pl.BlockSpec((tm, D), lambda i: (i, 0)),
                  pl.BlockSpec((tm, LANES), lambda i: (i, 0)),
                  pl.BlockSpec((None, 1, D), lambda i: (i // per_b, 0, 0)),
                  pl.BlockSpec((1, D), lambda i: (0, 0)),
                  pl.BlockSpec(memory_space=pl.ANY)],
        out_specs=pl.BlockSpec((tm, D), lambda i: (i, 0)),
        out_shape=jax.ShapeDtypeStruct((T, D), F32),
        scratch_shapes=[pltpu.VMEM((2, 2, tm, D), F32), pltpu.SemaphoreType.DMA((2,))],
        compiler_params=_cparams(("arbitrary",)),
        name="moe_combine",
    )(dest, dest, xf, route, g2, fnw, yb)


def _rot_cols(w):
    half = w.shape[1] // 2
    return jnp.concatenate([-w[:, half:], w[:, :half]], axis=1)


def _pad_lanes(a, n=LANES):
    return jnp.pad(a, [(0, 0)] * (a.ndim - 1) + [(0, n - a.shape[-1])])


def kernel(x, c, positions, w_ada, b_ada, norm1_w, w_in, mla_q_norm_w, mla_w_uq, mla_kv_norm_w, mla_w_ukv, diff_lambda, diff_subln_w, ssd_conv_w, ssd_conv_b, ssd_dt_bias, ssd_a_log, ssd_d, ssd_norm_w, w_out, norm2_w, router_w_group, router_b_group, router_w_expert, router_b_expert, exp_w_gate, exp_w_up, exp_w_down, final_norm_w):
    B, S, D = x.shape
    T = B * S
    xf = x.reshape(T, D)
    mod = _ada(jnp.pad(c, ((0, SUBLANES - B), (0, 0))), w_ada, b_ada)[:, :B]

    pos_f = positions.astype(F32)
    pos_col = pos_f.reshape(T, 1)
    inv = 1.0 / (ROPE_THETA ** (jnp.arange(0, MLA_D_ROPE, 2, dtype=F32) / MLA_D_ROPE))
    inv = jnp.tile(inv, 4).reshape(1, LANES)
    slopes = jnp.exp2(-8.0 / DIFF_HEADS * jnp.arange(1, DIFF_HEADS + 1, dtype=F32)).reshape(DIFF_HEADS, 1, 1)
    slopes = slopes * LOG2E
    tq = min(FLASH_TQ, S)
    tqd = min(FLASH_TQ_DIFF, S)
    pos_ordered = jnp.all(positions[:, 1:] >= positions[:, :-1])
    head_of_col = np.arange(SSD_D_INNER) // SSD_HEAD_DIM
    emat = jnp.asarray(np.arange(LANES)[:, None] == head_of_col[None, :], BF16)

    cap = T * 2 + N_EXPERTS * MOE_ROWS
    nb = cap // MOE_ROWS
    xb_zero = jnp.zeros((cap, D), F32)
    fnw = final_norm_w.reshape(1, D)

    for l in range(DEPTH):
        sh1, sc1, g1, sh2, sc2, g2 = [mod[l, :, i * D:(i + 1) * D].reshape(B, 1, D) for i in range(6)]
        w = w_in[l]
        ws = [jnp.concatenate([w[:, :_OFF[3]], _rot_cols(w[:, _OFF[2]:_OFF[3]])], axis=1)]
        ws.append(w[:, _OFF[3]:_OFF[4]] * (DIFF_D_QK ** -0.5 * LOG2E))
        ws += [w[:, _OFF[i]:_OFF[i + 1]] for i in range(4, 8)]
        ws.append(_pad_lanes(w[:, _OFF[8]:_OFF[9]]))
        ws = [a.astype(BF16) for a in ws]
        a_mla, dq, dk, dv, z, xbc, dt = _inproj(xf, norm1_w[l].reshape(1, D), sh1, sc1, ws, S)

        wq = mla_w_uq[l].reshape(MLA_Q_LORA, MLA_HEADS, MLA_D_NOPE + MLA_D_ROPE)
        wq_pe = wq[:, :, MLA_D_NOPE:]
        wq_rot = jnp.concatenate([-wq_pe[:, :, MLA_D_ROPE // 2:], wq_pe[:, :, :MLA_D_ROPE // 2]], axis=2)
        wq_cat = jnp.concatenate([wq, wq_rot], axis=2).reshape(MLA_Q_LORA, MLA_HEADS * 256).astype(BF16)
        q, k, v = _mla_prep(a_mla, pos_col, inv, mla_q_norm_w[l].reshape(1, -1), mla_kv_norm_w[l].reshape(1, -1),
                            wq_cat, mla_w_ukv[l].astype(BF16))
        q_t = q.transpose(0, 2, 1)
        v_t = v.reshape(MLA_HEADS, T // tq, tq, MLA_D_V).transpose(0, 1, 3, 2)
        y_mla = _mla_flash(q_t, k, v_t, B, S, tq)

        lam_init = 0.8 - 0.6 * math.exp(-0.3 * l)
        dv_t = dv.reshape(T // tqd, tqd, DIFF_HEADS * DIFF_D_V).transpose(0, 2, 1)
        diff_args = (dq.T, dk, dv_t, pos_f.reshape(B, 1, S), pos_f.reshape(T // tqd, tqd, 1), slopes,
                     diff_lambda[l], diff_subln_w[l].reshape(-1, 1))
        y_diff = lax.cond(pos_ordered,
                          lambda *a: _diff_flash(*a, B, S, tqd, lam_init, ordered=True),
                          lambda *a: _diff_flash(*a, B, S, tqd, lam_init, ordered=False), *diff_args)

        y_ssd = _ssd(xbc, z, dt, ssd_conv_w[l], ssd_conv_b[l].reshape(1, -1),
                     _pad_lanes(ssd_dt_bias[l].reshape(1, -1)), _pad_lanes(ssd_a_log[l].reshape(1, -1)),
                     jnp.repeat(ssd_d[l], SSD_HEAD_DIM).reshape(1, -1), ssd_norm_w[l].reshape(1, -1), emat, B, S)

        wo = w_out[l].astype(BF16)
        wr = _pad_lanes(jnp.concatenate([router_w_expert[l], router_w_group[l]], axis=1))
        br = _pad_lanes(jnp.concatenate([router_b_expert[l], router_b_group[l]]).reshape(1, -1))
        x_mid, h2, route, cnt = _outproj_router(xf, y_mla, y_diff, y_ssd, g1, sh2, sc2, norm2_w[l].reshape(1, D),
                                                wo[:512], wo[512:1024], wo[1024:], wr, br, S)

        counts = cnt[0, :N_EXPERTS].astype(jnp.int32)
        pcounts = (counts + MOE_ROWS - 1) // MOE_ROWS * MOE_ROWS
        pend = jnp.cumsum(pcounts)
        pstart = pend - pcounts
        eid = route[:, 0:2].astype(jnp.int32)
        start_of = jnp.sum(jnp.where(eid[..., None] == jnp.arange(N_EXPERTS), pstart, 0), axis=-1)
        dest = (start_of + route[:, 2:4].astype(jnp.int32)).reshape(-1)
        starts = jnp.arange(nb, dtype=jnp.int32) * MOE_ROWS
        bexp = jnp.minimum(jnp.sum(pend[None, :] <= starts[:, None], axis=1), N_EXPERTS - 1).astype(jnp.int32)
        nused = (pend[-1:] // MOE_ROWS).astype(jnp.int32)

        xb = _dispatch(dest, h2, xb_zero)
        yb = _experts(bexp, nused, xb, exp_w_gate, exp_w_up, exp_w_down, l)
        xf = _combine(dest, x_mid, route, g2, fnw, yb, S, final=(l == DEPTH - 1))
    return xf.reshape(B, S, D)
```

```python
import functools
import math

import numpy as np
import jax
import jax.numpy as jnp
from jax import lax
from jax.experimental import pallas as pl
from jax.experimental.pallas import tpu as pltpu

F32 = jnp.float32
BF16 = jnp.bfloat16
HIGHEST = lax.Precision.HIGHEST

D_MODEL = 1024
DEPTH = 2
MLA_HEADS = 4
MLA_Q_LORA = 256
MLA_KV_LORA = 128
MLA_D_NOPE = 128
MLA_D_ROPE = 64
MLA_D_V = 128
ROPE_THETA = 10000.0
DIFF_HEADS = 4
DIFF_D_QK = 64
DIFF_D_V = 128
SSD_HEADS = 16
SSD_HEAD_DIM = 64
SSD_D_INNER = 1024
SSD_GROUPS = 4
SSD_STATE = 128
SSD_CONV = 4
SSD_CHUNK = 128
SSD_CONV_DIM = SSD_D_INNER + 2 * SSD_GROUPS * SSD_STATE
MOE_GROUPS = 4
MOE_EXPERTS_PER_GROUP = 8
N_EXPERTS = 32
D_EXPERT = 512
NORM_EPS = 1e-6
SUBLN_EPS = 1e-5

LANES = 128
SUBLANES = 8
MOE_ROWS = 256
FLASH_TQ = 1024
FLASH_TQ_DIFF = 512
LOG2E = math.log2(math.e)
VMEM_LIMIT = 48 * 1024 * 1024

_OFF = np.cumsum([0, MLA_Q_LORA, MLA_KV_LORA, MLA_D_ROPE, 512, 512, 512, SSD_D_INNER, SSD_CONV_DIM, SSD_HEADS])


def _cparams(sem):
    return pltpu.CompilerParams(dimension_semantics=sem, vmem_limit_bytes=VMEM_LIMIT)


def _dot(a, b, **kw):
    return jnp.dot(a, b, preferred_element_type=F32, **kw)


def _dot_nt(a, b):
    return lax.dot_general(a, b, (((1,), (1,)), ((), ())), preferred_element_type=F32)


def _rms(x, eps):
    return x * lax.rsqrt(jnp.mean(x * x, axis=-1, keepdims=True) + eps)


def _sigmoid(x):
    return 1.0 / (1.0 + jnp.exp(-x))


def _split_bf16(x, n):
    pieces = []
    for _ in range(n):
        head = x.astype(BF16)
        pieces.append(head)
        x = x - head.astype(F32)
    return pieces


def _ada_kernel(c_ref, w_ref, b_ref, o_ref):
    c = c_ref[...]
    cond = c * _sigmoid(c)
    o_ref[...] = _dot(cond, w_ref[...], precision=HIGHEST) + b_ref[...]


def _ada(c_pad, w_ada, b_ada):
    L, D, N = w_ada.shape
    tn = 1024
    return pl.pallas_call(
        _ada_kernel,
        grid=(L, N // tn),
        in_specs=[
            pl.BlockSpec((SUBLANES, D), lambda l, j: (0, 0)),
            pl.BlockSpec((None, D, tn), lambda l, j: (l, 0, j)),
            pl.BlockSpec((None, 1, tn), lambda l, j: (l, 0, j)),
        ],
        out_specs=pl.BlockSpec((None, SUBLANES, tn), lambda l, j: (l, 0, j)),
        out_shape=jax.ShapeDtypeStruct((L, SUBLANES, N), F32),
        compiler_params=_cparams(("parallel", "parallel")),
        name="ada",
    )(c_pad, w_ada, b_ada.reshape(L, 1, N))


def _inproj_kernel(x_ref, nw_ref, sh_ref, sc_ref, wm_ref, wq_ref, wk_ref, wv_ref, wz_ref, wx_ref, wt_ref,
                   om_ref, oq_ref, ok_ref, ov_ref, oz_ref, ox_ref, ot_ref):
    x = x_ref[...]
    h = _rms(x, NORM_EPS) * nw_ref[...] * (1.0 + sc_ref[...]) + sh_ref[...]
    hb = h.astype(BF16)
    for w_ref, o_ref in ((wm_ref, om_ref), (wq_ref, oq_ref), (wk_ref, ok_ref), (wv_ref, ov_ref),
                         (wz_ref, oz_ref), (wx_ref, ox_ref), (wt_ref, ot_ref)):
        o_ref[...] = _dot(hb, w_ref[...]).astype(o_ref.dtype)


def _inproj(xf, nw, sh, sc, ws, seq):
    T, D = xf.shape
    tm = 256
    per_b = seq // tm
    widths = [w.shape[1] for w in ws]
    row = lambda i: (i, 0)
    const = lambda i: (0, 0)
    bat = lambda i: (i // per_b, 0, 0)
    in_specs = [pl.BlockSpec((tm, D), row), pl.BlockSpec((1, D), const),
                pl.BlockSpec((None, 1, D), bat), pl.BlockSpec((None, 1, D), bat)]
    in_specs += [pl.BlockSpec((D, n), const) for n in widths]
    out_dt = [BF16] * 6 + [F32]
    return pl.pallas_call(
        _inproj_kernel,
        grid=(T // tm,),
        in_specs=in_specs,
        out_specs=[pl.BlockSpec((tm, n), row) for n in widths],
        out_shape=[jax.ShapeDtypeStruct((T, n), dt) for n, dt in zip(widths, out_dt)],
        compiler_params=_cparams(("parallel",)),
        name="inproj",
    )(xf, nw, sh, sc, *ws)


def _mla_prep_kernel(a_ref, pos_ref, inv_ref, qnw_ref, kvnw_ref, wq_ref, wkv_ref, q_ref, k_ref, v_ref):
    a = a_ref[...].astype(F32)
    cqn = (_rms(a[:, :256], NORM_EPS) * qnw_ref[...]).astype(BF16)
    ckvn = (_rms(a[:, 256:384], NORM_EPS) * kvnw_ref[...]).astype(BF16)
    ang = pos_ref[...] * inv_ref[...]
    lane = lax.broadcasted_iota(jnp.int32, ang.shape, 1)
    cs = jnp.where(lane < MLA_D_ROPE, jnp.cos(ang), jnp.sin(ang))
    kp = a[:, 384:512] * cs
    krkr = kp + pltpu.roll(kp, MLA_D_ROPE, axis=1)
    scale = (MLA_D_NOPE + MLA_D_ROPE) ** -0.5 * LOG2E
    qall = _dot(cqn, wq_ref[...]) * scale
    kvall = _dot(ckvn, wkv_ref[...])
    for h in range(MLA_HEADS):
        o = h * 256
        q_ref[h] = jnp.concatenate([qall[:, o:o + 128], qall[:, o + 128:o + 256] * cs], axis=1).astype(BF16)
        k_ref[h] = jnp.concatenate([kvall[:, o:o + 128], krkr], axis=1).astype(BF16)
        v_ref[h] = kvall[:, o + 128:o + 256].astype(BF16)


def _mla_prep(a, pos_col, inv, qnw, kvnw, wq, wkv):
    T = a.shape[0]
    tm = 512
    row = lambda i: (i, 0)
    const = lambda i: (0, 0)
    hrow = lambda i: (0, i, 0)
    return pl.pallas_call(
        _mla_prep_kernel,
        grid=(T // tm,),
        in_specs=[pl.BlockSpec((tm, 512), row), pl.BlockSpec((tm, 1), row), pl.BlockSpec((1, LANES), const),
                  pl.BlockSpec((1, 256), const), pl.BlockSpec((1, 128), const),
                  pl.BlockSpec((256, 1024), const), pl.BlockSpec((128, 1024), const)],
        out_specs=[pl.BlockSpec((MLA_HEADS, tm, 256), hrow), pl.BlockSpec((MLA_HEADS, tm, 256), hrow),
                   pl.BlockSpec((MLA_HEADS, tm, 128), hrow)],
        out_shape=[jax.ShapeDtypeStruct((MLA_HEADS, T, 256), BF16), jax.ShapeDtypeStruct((MLA_HEADS, T, 256), BF16),
                   jax.ShapeDtypeStruct((MLA_HEADS, T, 128), BF16)],
        compiler_params=_cparams(("parallel",)),
        name="mla_prep",
    )(a, pos_col, inv, qnw, kvnw, wq, wkv)


FLASH_STRIP = LANES


def _flash_init(m_s, l_s, acc_s, p_s, a_s, s_s):
    del s_s
    m_s[...] = jnp.full(m_s.shape, -jnp.inf, F32)
    l_s[...] = jnp.zeros(l_s.shape, F32)
    acc_s[...] = jnp.zeros(acc_s.shape, F32)
    p_s[1] = jnp.zeros(p_s.shape[1:], BF16)
    a_s[1] = jnp.ones(a_s.shape[1:], F32)


def _flash_pv(v_t, slot, acc_s, p_s, a_s):
    acc_s[...] = a_s[slot] * acc_s[...] + _dot(v_t, p_s[slot])


def _flash_stage(next_scores, strip_fn, v_prev, cur, m_s, l_s, acc_s, p_s, a_s, s_s):
    if next_scores is not None:
        s_s[1 - cur] = next_scores()
    _flash_pv(v_prev, 1 - cur, acc_s, p_s, a_s)
    for j in range(m_s.shape[1] // FLASH_STRIP):
        sl = slice(j * FLASH_STRIP, (j + 1) * FLASH_STRIP)
        s, bias, visible = strip_fn(s_s, cur, j)
        keep = (lambda x: x) if visible is None else (lambda x: jnp.where(visible, x, -jnp.inf))
        m_old = m_s[:, sl]
        t = s if bias is None else s - bias
        m_new = jnp.maximum(m_old, jnp.max(keep(t), axis=0, keepdims=True))
        alpha = jnp.exp2(m_old - m_new)
        p = jnp.exp2(keep(t - m_new))
        l_s[:, sl] = alpha * l_s[:, sl] + jnp.sum(p, axis=0, keepdims=True)
        m_s[:, sl] = m_new
        p_s[cur, :, sl] = p.astype(BF16)
        a_s[cur, :, sl] = alpha


def _flash_run(qi, scores, stage, flush, s_s):
    if scores is not None:
        s_s[0] = scores(0)

    def pair(i, _):
        a = 2 * i
        stage(a + 1, a, False, 0)
        stage(a + 2, a + 1, False, 1)
        return 0

    lax.fori_loop(0, qi // 2, pair, 0)

    @pl.when(qi % 2 == 0)
    def _():
        stage(None, qi, True, 0)
        flush(qi, 0)

    @pl.when(qi % 2 == 1)
    def _():
        stage(qi, qi - 1, False, 0)
        stage(None, qi, True, 1)
        flush(qi, 1)


def _causal_strip(shape, j):
    r = lax.broadcasted_iota(jnp.int32, shape, 0)
    c = lax.broadcasted_iota(jnp.int32, shape, 1) + j * FLASH_STRIP
    return r <= c


def _flash_scratch(dv, tk, nq, score_slots=True):
    s_shape = (2, tk, nq) if score_slots else (2, SUBLANES, LANES)
    return [pltpu.VMEM((1, nq), F32), pltpu.VMEM((1, nq), F32), pltpu.VMEM((dv, nq), F32),
            pltpu.VMEM((2, tk, nq), BF16), pltpu.VMEM((2, 1, nq), F32), pltpu.VMEM(s_shape, F32)]


def _mla_flash_kernel(qt_ref, k_ref, vt_ref, o_ref, *state, tq):
    m_s, l_s, acc_s, p_s, a_s, s_s = state
    qi = pl.program_id(2)
    _flash_init(*state)
    q_t = qt_ref[...]

    def scores(kb):
        return _dot(k_ref[pl.ds(pl.multiple_of(kb * tq, tq), tq), :], q_t)

    def stage(nxt, kb, masked, cur):
        def strip(s_ref, slot, j):
            t = s_ref[slot, :, j * FLASH_STRIP:(j + 1) * FLASH_STRIP]
            return t, None, (_causal_strip(t.shape, j) if masked else None)

        _flash_stage(None if nxt is None else (lambda: scores(nxt)), strip, vt_ref[jnp.maximum(kb - 1, 0)], cur,
                     *state)

    _flash_run(qi, scores, stage, lambda kb, slot: _flash_pv(vt_ref[kb], slot, acc_s, p_s, a_s), s_s)
    o_ref[...] = (acc_s[...] * (1.0 / l_s[...])).T.astype(o_ref.dtype)


def _mla_flash(q_t, k, v_t, batch, seq, tq):
    nq = seq // tq
    return pl.pallas_call(
        functools.partial(_mla_flash_kernel, tq=tq),
        grid=(batch, MLA_HEADS, nq),
        in_specs=[pl.BlockSpec((None, 256, tq), lambda b, h, i: (h, 0, b * nq + i)),
                  pl.BlockSpec((None, seq, 256), lambda b, h, i: (h, b, 0)),
                  pl.BlockSpec((None, nq, MLA_D_V, tq), lambda b, h, i: (h, b, 0, 0))],
        out_specs=pl.BlockSpec((tq, MLA_D_V), lambda b, h, i: (b * nq + i, h)),
        out_shape=jax.ShapeDtypeStruct((batch * seq, MLA_HEADS * MLA_D_V), BF16),
        scratch_shapes=_flash_scratch(MLA_D_V, tq, tq),
        compiler_params=_cparams(("parallel", "parallel", "arbitrary")),
        name="mla_flash",
    )(q_t, k, v_t)


def _diff_flash_kernel(qt_ref, k_ref, vt_ref, pq_ref, pk_ref, slope_ref, lam_ref, sw_ref, o_ref,
                       m_s, l_s, acc_s, p_s, a_s, s_s, kp_s, *, tq, lam_init, ordered):
    state = (m_s, l_s, acc_s, p_s, a_s, s_s)
    qi = pl.program_id(2)
    _flash_init(*state)
    q_t = qt_ref[...]
    row = lax.broadcasted_iota(jnp.int32, q_t.shape, 0)
    zero = jnp.zeros_like(q_t)
    q2_t = jnp.concatenate([jnp.where(row < DIFF_D_QK, q_t, zero), jnp.where(row >= DIFF_D_QK, q_t, zero)], axis=1)
    spm = tq // FLASH_STRIP
    if ordered:
        ones = jnp.where(lax.broadcasted_iota(jnp.int32, q2_t.shape, 0) < 3, 1.0, 0.0).astype(BF16)
        q2_t = jnp.concatenate([q2_t, ones], axis=0)

        @pl.when(qi == 0)
        def _():
            lane = lax.broadcasted_iota(jnp.int32, (tq, LANES), 1)

            def fill(kb, _):
                pieces = _split_bf16(slope_ref[...] * pk_ref[kb], 3)
                tile = jnp.zeros((tq, LANES), F32)
                for i, piece in enumerate(pieces):
                    tile = jnp.where(lane == i, piece.astype(F32), tile)
                kp_s[pl.ds(pl.multiple_of(kb * tq, tq), tq), :] = tile.astype(BF16)
                return 0

            lax.fori_loop(0, pk_ref.shape[0], fill, 0)
    else:
        slope = slope_ref[...]
        spq = slope * pq_ref[...]

    def scores(kb):
        rows = pl.ds(pl.multiple_of(kb * tq, tq), tq)
        k = k_ref[rows, :]
        if ordered:
            k = jnp.concatenate([k, kp_s[rows, :]], axis=1)
        return _dot(k, q2_t)

    def stage(nxt, kb, masked, cur):
        del nxt
        s_t = scores(kb)
        if not ordered:
            bias = jnp.abs(slope * pk_ref[kb] - spq)

        def strip(s_ref, slot, j):
            jq = j % spm
            t = s_t[:, j * FLASH_STRIP:(j + 1) * FLASH_STRIP]
            return (t, None if ordered else bias[:, jq * FLASH_STRIP:(jq + 1) * FLASH_STRIP],
                    _causal_strip(t.shape, jq) if masked else None)

        _flash_stage(None, strip, vt_ref[jnp.maximum(kb - 1, 0)], cur, *state)

    _flash_run(qi, None, stage, lambda kb, slot: _flash_pv(vt_ref[kb], slot, acc_s, p_s, a_s), s_s)
    o2 = acc_s[...] * (1.0 / l_s[...])
    lf = lam_ref[...]
    lam = (jnp.exp(jnp.sum(lf[0:1] * lf[1:2], keepdims=True))
           - jnp.exp(jnp.sum(lf[2:3] * lf[3:4], keepdims=True)) + lam_init)
    out = o2[:, :tq] - lam * o2[:, tq:]
    out = out * lax.rsqrt(jnp.mean(out * out, axis=0, keepdims=True) + SUBLN_EPS) * sw_ref[...] * (1.0 - lam_init)
    o_ref[...] = out.T.astype(o_ref.dtype)


def _diff_flash(q_t, k, v_t, pos_row, pos_kcol, slopes, lam, sw_col, batch, seq, tq, lam_init, ordered):
    nq = seq // tq
    return pl.pallas_call(
        functools.partial(_diff_flash_kernel, tq=tq, lam_init=lam_init, ordered=ordered),
        grid=(batch, DIFF_HEADS, nq),
        in_specs=[pl.BlockSpec((128, tq), lambda b, h, i: (h, b * nq + i)),
                  pl.BlockSpec((seq, 128), lambda b, h, i: (b, h)),
                  pl.BlockSpec((nq, DIFF_D_V, tq), lambda b, h, i: (b, h, 0)),
                  pl.BlockSpec((None, 1, tq), lambda b, h, i: (b, 0, i)),
                  pl.BlockSpec((nq, tq, 1), lambda b, h, i: (b, 0, 0)),
                  pl.BlockSpec((None, 1, 1), lambda b, h, i: (h, 0, 0)),
                  pl.BlockSpec((4, DIFF_D_QK), lambda b, h, i: (0, 0)),
                  pl.BlockSpec((DIFF_D_V, 1), lambda b, h, i: (0, 0))],
        out_specs=pl.BlockSpec((tq, DIFF_D_V), lambda b, h, i: (b * nq + i, h)),
        out_shape=jax.ShapeDtypeStruct((batch * seq, DIFF_HEADS * DIFF_D_V), BF16),
        scratch_shapes=_flash_scratch(DIFF_D_V, tq, 2 * tq, score_slots=False)
        + [pltpu.VMEM((seq, LANES) if ordered else (SUBLANES * 2, LANES), BF16)],
        compiler_params=_cparams(("parallel", "parallel", "arbitrary")),
        name="diff_flash",
    )(q_t, k, v_t, pos_row, pos_kcol, slopes, lam, sw_col)


def _ssd_kernel(xbc_ref, z_ref, dt_ref, cw_ref, cb_ref, dtb_ref, alog_ref, dsk_ref, nw_ref, e_ref, o_ref,
                ext, state):
    L = SSD_CHUNK
    G, N = SSD_GROUPS, SSD_STATE
    GW = SSD_D_INNER // G

    @pl.when(pl.program_id(1) == 0)
    def _():
        ext[0:SUBLANES, :] = jnp.zeros((SUBLANES, SSD_CONV_DIM), F32)
        state[...] = jnp.zeros(state.shape, F32)

    u = xbc_ref[...].astype(F32)
    ext[SUBLANES:SUBLANES + L, :] = u
    acc = cb_ref[...] + cw_ref[SSD_CONV - 1:SSD_CONV, :] * u
    for k in range(SSD_CONV - 1):
        o = SUBLANES - (SSD_CONV - 1) + k
        acc = acc + cw_ref[k:k + 1, :] * ext[o:o + L, :]
    ext[0:SUBLANES, :] = u[L - SUBLANES:L, :]
    act = acc * _sigmoid(acc)
    xs = act[:, :SSD_D_INNER]
    bm = act[:, SSD_D_INNER:SSD_D_INNER + G * N]
    cm = act[:, SSD_D_INNER + G * N:]

    dtr = dt_ref[...] + dtb_ref[...]
    dtv = jnp.maximum(dtr, 0.0) + jnp.log1p(jnp.exp(-jnp.abs(dtr)))
    adt = dtv * (-jnp.exp(alog_ref[...]))
    r = lax.broadcasted_iota(jnp.int32, (L, L), 0)
    c = lax.broadcasted_iota(jnp.int32, (L, L), 1)
    tril = r >= c
    a_cs = _dot(tril.astype(F32), adt, precision=HIGHEST)
    a_cs_t = a_cs.T
    ea = jnp.exp(a_cs)
    dte = jnp.exp(a_cs[L - 1:L, :] - a_cs)
    stack = jnp.concatenate([dtv, dte, ea], axis=0)
    hi, lo = _split_bf16(stack, 2)
    ex = _dot(hi, e_ref[...]) + _dot(lo, e_ref[...])
    dt_e, dte_e, ea_e = ex[:L], ex[L:2 * L], ex[2 * L:]
    xdt = xs * dt_e
    xdt_b = xdt.astype(BF16)
    xw_b = (xdt * dte_e).astype(BF16)
    bb = bm.astype(BF16)
    cbf = cm.astype(BF16)
    lane_g = lax.broadcasted_iota(jnp.int32, (1, GW), 1)
    ys = []
    for g in range(G):
        bg = bb[:, g * N:(g + 1) * N]
        cg = cbf[:, g * N:(g + 1) * N]
        cb = _dot_nt(cg, bg)
        xg = xdt_b[:, g * GW:(g + 1) * GW]
        yd = jnp.zeros((L, GW), F32)
        for rr in range(SSD_HEADS // G):
            h = g * (SSD_HEADS // G) + rr
            seg = a_cs[:, h:h + 1] - a_cs_t[h:h + 1, :]
            dec = jnp.exp(jnp.where(tril, seg, -jnp.inf))
            in_head = (lane_g >= rr * SSD_HEAD_DIM) & (lane_g < (rr + 1) * SSD_HEAD_DIM)
            yd = yd + _dot((cb * dec).astype(BF16), jnp.where(in_head, xg, jnp.zeros_like(xg)))
        st_old = state[g]
        eg = ea_e[:, g * GW:(g + 1) * GW]
        yoff = _dot(cg, st_old.astype(BF16)) * eg
        bg_t = bm[:, g * N:(g + 1) * N].T.astype(BF16)
        state[g] = st_old * eg[L - 1:L, :] + _dot(bg_t, xw_b[:, g * GW:(g + 1) * GW])
        ys.append(yd + yoff)
    y = jnp.concatenate(ys, axis=1) + xs * dsk_ref[...]
    zf = z_ref[...].astype(F32)
    y = y * (zf * _sigmoid(zf))
    y = jnp.concatenate([_rms(y[:, g * GW:(g + 1) * GW], NORM_EPS) for g in range(G)], axis=1)
    o_ref[...] = (y * nw_ref[...]).astype(o_ref.dtype)


def _ssd(xbc, z, dt, cw, cb, dtb, alog, dsk, nw, emat, batch, seq):
    L = SSD_CHUNK
    nc = seq // L
    row = lambda b, c: (b * nc + c, 0)
    const = lambda b, c: (0, 0)
    return pl.pallas_call(
        _ssd_kernel,
        grid=(batch, nc),
        in_specs=[pl.BlockSpec((L, SSD_CONV_DIM), row), pl.BlockSpec((L, SSD_D_INNER), row),
                  pl.BlockSpec((L, LANES), row),
                  pl.BlockSpec((SSD_CONV, SSD_CONV_DIM), const), pl.BlockSpec((1, SSD_CONV_DIM), const),
                  pl.BlockSpec((1, LANES), const), pl.BlockSpec((1, LANES), const),
                  pl.BlockSpec((1, SSD_D_INNER), const), pl.BlockSpec((1, SSD_D_INNER), const),
                  pl.BlockSpec((LANES, SSD_D_INNER), const)],
        out_specs=pl.BlockSpec((L, SSD_D_INNER), row),
        out_shape=jax.ShapeDtypeStruct((batch * seq, SSD_D_INNER), BF16),
        scratch_shapes=[pltpu.VMEM((SUBLANES + L, SSD_CONV_DIM), F32),
                        pltpu.VMEM((SSD_GROUPS, SSD_STATE, SSD_D_INNER // SSD_GROUPS), F32)],
        compiler_params=_cparams(("arbitrary", "arbitrary")),
        name="ssd",
    )(xbc, z, dt, cw, cb, dtb, alog, dsk, nw, emat)


def _outproj_router_kernel(x_ref, ym_ref, yd_ref, ys_ref, g1_ref, sh_ref, sc_ref, nw_ref, wm_ref, wd_ref, ws_ref,
                           wr_ref, br_ref, xo_ref, h_ref, route_ref, cnt_ref, carry, wr_s):
    @pl.when(pl.program_id(0) == 0)
    def _():
        carry[...] = jnp.zeros(carry.shape, F32)
        wr_s[0], wr_s[1] = _split_bf16(wr_ref[...], 2)

    y = _dot(ym_ref[...], wm_ref[...]) + _dot(yd_ref[...], wd_ref[...]) + _dot(ys_ref[...], ws_ref[...])
    xn = x_ref[...] + g1_ref[...] * y
    xo_ref[...] = xn
    h = _rms(xn, NORM_EPS) * nw_ref[...] * (1.0 + sc_ref[...]) + sh_ref[...]
    h_ref[...] = h
    h_hi, h_lo = _split_bf16(h, 2)
    logits = _dot(h_hi, wr_s[0]) + (_dot(h_lo, wr_s[0]) + _dot(h_hi, wr_s[1])) + br_ref[...]
    tm = logits.shape[0]
    lane = lax.broadcasted_iota(jnp.int32, logits.shape, 1)
    big = jnp.int32(1 << 20)
    neg = -jnp.inf
    lg = jnp.where((lane >= N_EXPERTS) & (lane < N_EXPERTS + MOE_GROUPS), logits, neg)
    mg = jnp.max(lg, axis=-1, keepdims=True)
    pg_top = 1.0 / jnp.sum(jnp.exp(lg - mg), axis=-1, keepdims=True)
    gsel = jnp.min(jnp.where(lg == mg, lane, big), axis=-1, keepdims=True) - N_EXPERTS
    le = jnp.where((lane < N_EXPERTS) & ((lane // MOE_EXPERTS_PER_GROUP) == gsel), logits, neg)
    m1 = jnp.max(le, axis=-1, keepdims=True)
    e1 = jnp.min(jnp.where(le == m1, lane, big), axis=-1, keepdims=True)
    le2 = jnp.where(lane == e1, neg, le)
    m2 = jnp.max(le2, axis=-1, keepdims=True)
    e2 = jnp.min(jnp.where(le2 == m2, lane, big), axis=-1, keepdims=True)
    rr = jnp.exp(m2 - m1)
    gate1 = pg_top / (1.0 + rr)
    gate2 = pg_top * rr / (1.0 + rr)
    onehot = jnp.where((lane == e1) | (lane == e2), 1.0, 0.0)
    ri = lax.broadcasted_iota(jnp.int32, (tm, tm), 0)
    ci = lax.broadcasted_iota(jnp.int32, (tm, tm), 1)
    before = jnp.where(ri > ci, 1.0, 0.0).astype(BF16)
    prefix = _dot(before, onehot.astype(BF16)) + carry[...]
    rank1 = jnp.sum(jnp.where(lane == e1, prefix, 0.0), axis=-1, keepdims=True)
    rank2 = jnp.sum(jnp.where(lane == e2, prefix, 0.0), axis=-1, keepdims=True)
    carry[...] = carry[...] + jnp.sum(onehot, axis=0, keepdims=True)
    cnt_ref[...] = carry[...]
    cols = (e1.astype(F32), e2.astype(F32), rank1, rank2, gate1, gate2)
    route = jnp.zeros(logits.shape, F32)
    for i, col in enumerate(cols):
        route = jnp.where(lane == i, col, route)
    route_ref[...] = route


def _outproj_router(xf, ym, yd, ys, g1, sh2, sc2, n2w, wom, wod, wos, wr, br, seq):
    T, D = xf.shape
    tm = 512
    per_b = seq // tm
    row = lambda i: (i, 0)
    const = lambda i: (0, 0)
    bat = lambda i: (i // per_b, 0, 0)
    return pl.pallas_call(
        _outproj_router_kernel,
        grid=(T // tm,),
        in_specs=[pl.BlockSpec((tm, D), row), pl.BlockSpec((tm, 512), row), pl.BlockSpec((tm, 512), row),
                  pl.BlockSpec((tm, SSD_D_INNER), row),
                  pl.BlockSpec((None, 1, D), bat), pl.BlockSpec((None, 1, D), bat), pl.BlockSpec((None, 1, D), bat),
                  pl.BlockSpec((1, D), const),
                  pl.BlockSpec((512, D), const), pl.BlockSpec((512, D), const), pl.BlockSpec((SSD_D_INNER, D), const),
                  pl.BlockSpec((D, LANES), const), pl.BlockSpec((1, LANES), const)],
        out_specs=[pl.BlockSpec((tm, D), row), pl.BlockSpec((tm, D), row), pl.BlockSpec((tm, LANES), row),
                   pl.BlockSpec((1, LANES), const)],
        out_shape=[jax.ShapeDtypeStruct((T, D), F32), jax.ShapeDtypeStruct((T, D), F32),
                   jax.ShapeDtypeStruct((T, LANES), F32), jax.ShapeDtypeStruct((1, LANES), F32)],
        scratch_shapes=[pltpu.VMEM((1, LANES), F32), pltpu.VMEM((2, D, LANES), BF16)],
        compiler_params=_cparams(("arbitrary",)),
        name="outproj_router",
    )(xf, ym, yd, ys, g1, sh2, sc2, n2w, wom, wod, wos, wr, br)


def _row_copy(src, dst, i_src, i_dst, sem):
    return pltpu.make_async_copy(src.at[pl.ds(i_src, 1)], dst.at[pl.ds(i_dst, 1)], sem)


def _dispatch_kernel(dest_ref, h_ref, xb_in_ref, xb_ref, sem):
    del xb_in_ref
    tm = h_ref.shape[0]

    def issue(r, _):
        for k in range(2):
            _row_copy(h_ref, xb_ref, r, dest_ref[2 * r + k], sem).start()
        return 0

    lax.fori_loop(0, tm, issue, 0, unroll=8)

    def drain(r, _):
        for k in range(2):
            _row_copy(h_ref, xb_ref, 0, 0, sem).wait()
        return 0

    lax.fori_loop(0, tm, drain, 0, unroll=8)


def _dispatch(dest, h, xb_zero):
    T, D = h.shape
    tm = 256
    return pl.pallas_call(
        _dispatch_kernel,
        grid=(T // tm,),
        in_specs=[pl.BlockSpec((2 * tm,), lambda i: (i,), memory_space=pltpu.SMEM),
                  pl.BlockSpec((tm, D), lambda i: (i, 0)),
                  pl.BlockSpec(memory_space=pl.ANY)],
        out_specs=pl.BlockSpec(memory_space=pl.ANY),
        out_shape=jax.ShapeDtypeStruct(xb_zero.shape, F32),
        scratch_shapes=[pltpu.SemaphoreType.DMA(())],
        input_output_aliases={2: 0},
        compiler_params=_cparams(("arbitrary",)),
        name="moe_dispatch",
    )(dest, h, xb_zero)


def _expert_kernel(bexp_ref, nused_ref, x_ref, wg_ref, wu_ref, wd_ref, o_ref, wg_s, wu_s, wd_s):
    j = pl.program_id(0)
    used = j < nused_ref[0]
    new_expert = (j == 0) | (bexp_ref[j] != bexp_ref[jnp.maximum(j - 1, 0)])

    @pl.when(used & new_expert)
    def _():
        wg_s[...] = wg_ref[...].astype(BF16)
        wu_s[...] = wu_ref[...].astype(BF16)
        wd_s[...] = wd_ref[...].astype(BF16)

    @pl.when(used)
    def _():
        x = x_ref[...].astype(BF16)
        g = _dot(x, wg_s[...])
        u = _dot(x, wu_s[...])
        o_ref[...] = _dot((g * _sigmoid(g) * u).astype(BF16), wd_s[...])

    @pl.when(jnp.logical_not(used))
    def _():
        o_ref[...] = jnp.zeros(o_ref.shape, F32)


def _experts(bexp, nused, xb, w_gate, w_up, w_down, layer):
    cap, D = xb.shape
    nb = cap // MOE_ROWS
    F = D_EXPERT
    last = lambda j, be, nu: jnp.maximum(jnp.minimum(j, nu[0] - 1), 0)
    return pl.pallas_call(
        _expert_kernel,
        grid_spec=pltpu.PrefetchScalarGridSpec(
            num_scalar_prefetch=2,
            grid=(nb,),
            in_specs=[pl.BlockSpec((MOE_ROWS, D), lambda j, be, nu: (last(j, be, nu), 0)),
                      pl.BlockSpec((None, None, D, F), lambda j, be, nu: (layer, be[last(j, be, nu)], 0, 0)),
                      pl.BlockSpec((None, None, D, F), lambda j, be, nu: (layer, be[last(j, be, nu)], 0, 0)),
                      pl.BlockSpec((None, None, F, D), lambda j, be, nu: (layer, be[last(j, be, nu)], 0, 0))],
            out_specs=pl.BlockSpec((MOE_ROWS, D), lambda j, be, nu: (j, 0)),
            scratch_shapes=[pltpu.VMEM((D, F), BF16), pltpu.VMEM((D, F), BF16), pltpu.VMEM((F, D), BF16)],
        ),
        out_shape=jax.ShapeDtypeStruct((cap, D), F32),
        compiler_params=_cparams(("arbitrary",)),
        name="moe_experts",
    )(bexp, nused, xb, w_gate, w_up, w_down)


def _combine_kernel(dest_ref, dest_next_ref, x_ref, route_ref, g2_ref, fnw_ref, yb_ref, o_ref, gbuf, sem, *, final):
    tm = x_ref.shape[0]
    i = pl.program_id(0)
    slot = i % 2

    def issue(d_ref, s):
        def body(r, _):
            for k in range(2):
                _row_copy(yb_ref, gbuf.at[s, k], d_ref[2 * r + k], r, sem.at[s]).start()
            return 0

        lax.fori_loop(0, tm, body, 0, unroll=8)

    @pl.when(i == 0)
    def _():
        issue(dest_ref, 0)

    @pl.when(i + 1 < pl.num_programs(0))
    def _():
        issue(dest_next_ref, 1 - slot)

    def drain(r, _):
        for k in range(2):
            _row_copy(yb_ref, gbuf.at[slot, k], 0, 0, sem.at[slot]).wait()
        return 0

    lax.fori_loop(0, tm, drain, 0, unroll=8)
    route = route_ref[...]
    y = route[:, 4:5] * gbuf[slot, 0] + route[:, 5:6] * gbuf[slot, 1]
    xo = x_ref[...] + g2_ref[...] * y
    if final:
        xo = _rms(xo, NORM_EPS) * fnw_ref[...]
    o_ref[...] = xo


def _combine(dest, xf, route, g2, fnw, yb, seq, final):
    T, D = xf.shape
    tm = 256
    per_b = seq // tm
    last = T // tm - 1
    return pl.pallas_call(
        functools.partial(_combine_kernel, final=final),
        grid=(T // tm,),
        in_specs=[pl.BlockSpec((2 * tm,), lambda i: (i,), memory_space=pltpu.SMEM),
                  pl.BlockSpec((2 * tm,), lambda i: (jnp.minimum(i + 1, last),), memory_space=pltpu.SMEM),
                  pl.BlockSpec((tm, D), lambda i: (i, 0)),
                  pl.BlockSpec((tm, LANES), lambda i: (i, 0)),
                  pl.BlockSpec((None, 1, D), lambda i: (i // per_b, 0, 0)),
                  pl.BlockSpec((1, D), lambda i: (0, 0)),
                  pl.BlockSpec(memory_space=pl.ANY)],
        out_specs=pl.BlockSpec((tm, D), lambda i: (i, 0)),
        out_shape=jax.ShapeDtypeStruct((T, D), F32),
        scratch_shapes=[pltpu.VMEM((2, 2, tm, D), F32), pltpu.SemaphoreType.DMA((2,))],
        compiler_params=_cparams(("arbitrary",)),
        name="moe_combine",
    )(dest, dest, xf, route, g2, fnw, yb)


def _rot_cols(w):
    half = w.shape[1] // 2
    return jnp.concatenate([-w[:, half:], w[:, :half]], axis=1)


def _pad_lanes(a, n=LANES):
    return jnp.pad(a, [(0, 0)] * (a.ndim - 1) + [(0, n - a.shape[-1])])


def kernel(x, c, positions, w_ada, b_ada, norm1_w, w_in, mla_q_norm_w, mla_w_uq, mla_kv_norm_w, mla_w_ukv, diff_lambda, diff_subln_w, ssd_conv_w, ssd_conv_b, ssd_dt_bias, ssd_a_log, ssd_d, ssd_norm_w, w_out, norm2_w, router_w_group, router_b_group, router_w_expert, router_b_expert, exp_w_gate, exp_w_up, exp_w_down, final_norm_w):
    B, S, D = x.shape
    T = B * S
    xf = x.reshape(T, D)
    mod = _ada(jnp.pad(c, ((0, SUBLANES - B), (0, 0))), w_ada, b_ada)[:, :B]

    pos_f = positions.astype(F32)
    pos_col = pos_f.reshape(T, 1)
    inv = 1.0 / (ROPE_THETA ** (jnp.arange(0, MLA_D_ROPE, 2, dtype=F32) / MLA_D_ROPE))
    inv = jnp.tile(inv, 4).reshape(1, LANES)
    slopes = jnp.exp2(-8.0 / DIFF_HEADS * jnp.arange(1, DIFF_HEADS + 1, dtype=F32)).reshape(DIFF_HEADS, 1, 1)
    slopes = slopes * LOG2E
    tq = min(FLASH_TQ, S)
    tqd = min(FLASH_TQ_DIFF, S)
    pos_ordered = jnp.all(positions[:, 1:] >= positions[:, :-1])
    head_of_col = np.arange(SSD_D_INNER) // SSD_HEAD_DIM
    emat = jnp.asarray(np.arange(LANES)[:, None] == head_of_col[None, :], BF16)

    cap = T * 2 + N_EXPERTS * MOE_ROWS
    nb = cap // MOE_ROWS
    xb_zero = jnp.zeros((cap, D), F32)
    fnw = final_norm_w.reshape(1, D)

    for l in range(DEPTH):
        sh1, sc1, g1, sh2, sc2, g2 = [mod[l, :, i * D:(i + 1) * D].reshape(B, 1, D) for i in range(6)]
        w = w_in[l]
        ws = [jnp.concatenate([w[:, :_OFF[3]], _rot_cols(w[:, _OFF[2]:_OFF[3]])], axis=1)]
        ws.append(w[:, _OFF[3]:_OFF[4]] * (DIFF_D_QK ** -0.5 * LOG2E))
        ws += [w[:, _OFF[i]:_OFF[i + 1]] for i in range(4, 8)]
        ws.append(_pad_lanes(w[:, _OFF[8]:_OFF[9]]))
        ws = [a.astype(BF16) for a in ws]
        a_mla, dq, dk, dv, z, xbc, dt = _inproj(xf, norm1_w[l].reshape(1, D), sh1, sc1, ws, S)

        wq = mla_w_uq[l].reshape(MLA_Q_LORA, MLA_HEADS, MLA_D_NOPE + MLA_D_ROPE)
        wq_pe = wq[:, :, MLA_D_NOPE:]
        wq_rot = jnp.concatenate([-wq_pe[:, :, MLA_D_ROPE // 2:], wq_pe[:, :, :MLA_D_ROPE // 2]], axis=2)
        wq_cat = jnp.concatenate([wq, wq_rot], axis=2).reshape(MLA_Q_LORA, MLA_HEADS * 256).astype(BF16)
        q, k, v = _mla_prep(a_mla, pos_col, inv, mla_q_norm_w[l].reshape(1, -1), mla_kv_norm_w[l].reshape(1, -1),
                            wq_cat, mla_w_ukv[l].astype(BF16))
        q_t = q.transpose(0, 2, 1)
        v_t = v.reshape(MLA_HEADS, T // tq, tq, MLA_D_V).transpose(0, 1, 3, 2)
        y_mla = _mla_flash(q_t, k, v_t, B, S, tq)

        lam_init = 0.8 - 0.6 * math.exp(-0.3 * l)
        dv_t = dv.reshape(T // tqd, tqd, DIFF_HEADS * DIFF_D_V).transpose(0, 2, 1)
        diff_args = (dq.T, dk, dv_t, pos_f.reshape(B, 1, S), pos_f.reshape(T // tqd, tqd, 1), slopes,
                     diff_lambda[l], diff_subln_w[l].reshape(-1, 1))
        y_diff = lax.cond(pos_ordered,
                          lambda *a: _diff_flash(*a, B, S, tqd, lam_init, ordered=True),
                          lambda *a: _diff_flash(*a, B, S, tqd, lam_init, ordered=False), *diff_args)

        y_ssd = _ssd(xbc, z, dt, ssd_conv_w[l], ssd_conv_b[l].reshape(1, -1),
                     _pad_lanes(ssd_dt_bias[l].reshape(1, -1)), _pad_lanes(ssd_a_log[l].reshape(1, -1)),
                     jnp.repeat(ssd_d[l], SSD_HEAD_DIM).reshape(1, -1), ssd_norm_w[l].reshape(1, -1), emat, B, S)

        wo = w_out[l].astype(BF16)
        wr = _pad_lanes(jnp.concatenate([router_w_expert[l], router_w_group[l]], axis=1))
        br = _pad_lanes(jnp.concatenate([router_b_expert[l], router_b_group[l]]).reshape(1, -1))
        x_mid, h2, route, cnt = _outproj_router(xf, y_mla, y_diff, y_ssd, g1, sh2, sc2, norm2_w[l].reshape(1, D),
                                                wo[:512], wo[512:1024], wo[1024:], wr, br, S)

        counts = cnt[0, :N_EXPERTS].astype(jnp.int32)
        pcounts = (counts + MOE_ROWS - 1) // MOE_ROWS * MOE_ROWS
        pend = jnp.cumsum(pcounts)
        pstart = pend - pcounts
        eid = route[:, 0:2].astype(jnp.int32)
        start_of = jnp.sum(jnp.where(eid[..., None] == jnp.arange(N_EXPERTS), pstart, 0), axis=-1)
        dest = (start_of + route[:, 2:4].astype(jnp.int32)).reshape(-1)
        starts = jnp.arange(nb, dtype=jnp.int32) * MOE_ROWS
        bexp = jnp.minimum(jnp.sum(pend[None, :] <= starts[:, None], axis=1), N_EXPERTS - 1).astype(jnp.int32)
        nused = (pend[-1:] // MOE_ROWS).astype(jnp.int32)

        xb = _dispatch(dest, h2, xb_zero if l == 0 else xb)
        yb = _experts(bexp, nused, xb, exp_w_gate, exp_w_up, exp_w_down, l)
        xf = _combine(dest, x_mid, route, g2, fnw, yb, S, final=(l == DEPTH - 1))
    return xf.reshape(B, S, D)
```
